```python
import math
import jax, jax.numpy as jnp
from jax import lax
import numpy as np

D_MODEL = 2048
BATCH = 16
SEQ = 2048
DEPTH = 4

N_MIXERS = 4
D_FF = 4 * D_MODEL
NORM_EPS = 1e-6
FNET_GROUPS = 8
FNET_GC = D_MODEL // FNET_GROUPS
CONV_WIDTH = 31
GRID_W = 64
NA_HEADS = 16
NA_HEAD_DIM = D_MODEL // NA_HEADS
NA_WIN_H = 8
NA_WIN_W = 16
NA_QCOLS = 16
NA_KBAND = 2 * NA_WIN_W
S5_GC = 16
S5_GROUPS = D_MODEL // S5_GC
S5_STATE = 64

kernel_name = "hybrid_fnet_conformer_natten_s5_encoder"


def _rms_norm(x, g):
    xf = x.astype(jnp.float32)
    y = xf * lax.rsqrt(jnp.mean(jnp.square(xf), axis=-1, keepdims=True) + NORM_EPS)
    return (y * g.astype(jnp.float32)).astype(x.dtype)


def _layer_norm(x, g, b):
    xf = x.astype(jnp.float32)
    mu = jnp.mean(xf, axis=-1, keepdims=True)
    xc = xf - mu
    y = xc * lax.rsqrt(jnp.mean(jnp.square(xc), axis=-1, keepdims=True) + NORM_EPS)
    return (y * g.astype(jnp.float32) + b.astype(jnp.float32)).astype(x.dtype)


def _fourier_mixer(h, w, b):
    B_, S_, D_ = h.shape
    hf = h.astype(jnp.float32).reshape(B_, S_, FNET_GROUPS, FNET_GC)
    z = jnp.fft.fft2(hf, axes=(1, 3), norm="ortho").real
    z = z.reshape(B_, S_, D_).astype(h.dtype)
    return z @ w + b


def _conv_module(h, w_in, b_in, w_dw, b_dw, ln_g, ln_b, w_out, b_out):
    a, gate = jnp.split(h @ w_in + b_in, 2, axis=-1)
    u = a * jax.nn.sigmoid(gate)
    pad = CONV_WIDTH // 2
    u = lax.conv_general_dilated(
        u, w_dw[:, None, :].astype(u.dtype), window_strides=(1,),
        padding=[(pad, pad)], dimension_numbers=("NWC", "WIO", "NWC"),
        feature_group_count=u.shape[-1]) + b_dw
    u = jax.nn.silu(_layer_norm(u, ln_g, ln_b))
    return u @ w_out + b_out


def _na_column_tables():
    ncb = GRID_W // NA_QCOLS
    qcol = np.arange(GRID_W)
    cstart = np.clip(qcol - NA_WIN_W // 2, 0, GRID_W - NA_WIN_W)
    kb = np.clip(np.arange(ncb) * NA_QCOLS - NA_WIN_W // 2, 0, GRID_W - NA_KBAND)
    kcols = kb[:, None] + np.arange(NA_KBAND)[None, :]
    qc = qcol.reshape(ncb, NA_QCOLS)[:, :, None]
    cs = cstart.reshape(ncb, NA_QCOLS)[:, :, None]
    kc = kcols[:, None, :]
    valid = (kc >= cs) & (kc < cs + NA_WIN_W)
    dc_idx = np.clip(kc - qc + NA_WIN_W - 1, 0, 2 * NA_WIN_W - 2)
    return kcols, valid, dc_idx


def _neighborhood_attention(h, w_qkv, q_gain, k_gain, rpb, w_o):
    B_, S_, D_ = h.shape
    rows = S_ // GRID_W
    kh = min(NA_WIN_H, rows)
    ncb = GRID_W // NA_QCOLS
    kcols, valid, dc_idx = _na_column_tables()
    qkv = (h @ w_qkv).reshape(B_, rows, GRID_W, 3, NA_HEADS, NA_HEAD_DIM)
    q = _rms_norm(qkv[..., 0, :, :], q_gain) * (NA_HEAD_DIM ** -0.5)
    k = _rms_norm(qkv[..., 1, :, :], k_gain)
    v = qkv[..., 2, :, :]
    q = q.reshape(B_, rows, ncb, NA_QCOLS, NA_HEADS, NA_HEAD_DIM)
    rpb_col = rpb[:, :, dc_idx].astype(jnp.float32)
    mask = jnp.asarray(valid)[:, :, None, :]

    def row_step(r):
        r0 = jnp.clip(r - kh // 2, 0, rows - kh)
        q_r = lax.dynamic_index_in_dim(q, r, axis=1, keepdims=False)
        k_r = lax.dynamic_slice_in_dim(k, r0, kh, axis=1)[:, :, kcols]
        v_r = lax.dynamic_slice_in_dim(v, r0, kh, axis=1)[:, :, kcols]
        s = jnp.einsum("bnihd,bknjhd->bhnikj", q_r, k_r,
                       preferred_element_type=jnp.float32)
        dr_idx = r0 + jnp.arange(kh) - r + NA_WIN_H - 1
        bias = jnp.take(rpb_col, dr_idx, axis=1)
        s = s + jnp.transpose(bias, (0, 2, 3, 1, 4))
        s = jnp.where(mask, s, -jnp.inf)
        p = jax.nn.softmax(s.reshape(s.shape[:4] + (kh * NA_KBAND,)), axis=-1)
        p = p.reshape(s.shape).astype(v_r.dtype)
        o = jnp.einsum("bhnikj,bknjhd->bnihd", p, v_r)
        return o.reshape(B_, GRID_W, NA_HEADS, NA_HEAD_DIM)

    o = lax.map(row_step, jnp.arange(rows))
    o = jnp.transpose(o, (1, 0, 2, 3, 4)).reshape(B_, S_, D_)
    return o @ w_o


def _complex_affine_combine(e1, e2):
    a1r, a1i, b1r, b1i = e1
    a2r, a2i, b2r, b2i = e2
    ar = a2r * a1r - a2i * a1i
    ai = a2r * a1i + a2i * a1r
    br = a2r * b1r - a2i * b1i + b2r
    bi = a2r * b1i + a2i * b1r + b2i
    return (ar, ai, br, bi)


def _s5_direction(u, a_re, a_im, log_dt, b_re, b_im, c_re, c_im, reverse):
    f32 = jnp.float32
    lam_re = jnp.minimum(a_re.astype(f32), -1e-4)
    lam_im = a_im.astype(f32)
    dt = jnp.exp(log_dt.astype(f32))[:, None]
    mag = jnp.exp(lam_re * dt)
    ab_r = mag * jnp.cos(lam_im * dt)
    ab_i = mag * jnp.sin(lam_im * dt)
    den = jnp.square(lam_re) + jnp.square(lam_im)
    w_r = ab_r - 1.0
    w_i = ab_i
    f_r = (w_r * lam_re + w_i * lam_im) / den
    f_i = (w_i * lam_re - w_r * lam_im) / den
    br, bi = b_re.astype(f32), b_im.astype(f32)
    bb_r = f_r[..., None] * br - f_i[..., None] * bi
    bb_i = f_r[..., None] * bi + f_i[..., None] * br
    bu_r = jnp.einsum("bsgc,gpc->bsgp", u, bb_r)
    bu_i = jnp.einsum("bsgc,gpc->bsgp", u, bb_i)
    S_ = u.shape[1]
    a_r = jnp.broadcast_to(ab_r, (1, S_) + ab_r.shape)
    a_i = jnp.broadcast_to(ab_i, (1, S_) + ab_i.shape)
    _, _, x_r, x_i = lax.associative_scan(
        _complex_affine_combine, (a_r, a_i, bu_r, bu_i), reverse=reverse, axis=1)
    return (jnp.einsum("bsgp,gcp->bsgc", x_r, c_re.astype(f32))
            - jnp.einsum("bsgp,gcp->bsgc", x_i, c_im.astype(f32)))


def _s5_mixer(h, w_in, a_re, a_im, log_dt, b_re, b_im, c_re, c_im, d_skip, w_glu):
    B_, S_, D_ = h.shape
    u = (h @ w_in).astype(jnp.float32).reshape(B_, S_, S5_GROUPS, S5_GC)
    y = d_skip.astype(jnp.float32) * u
    for direction in range(2):
        y = y + _s5_direction(u, a_re[direction], a_im[direction], log_dt[direction],
                              b_re[direction], b_im[direction], c_re[direction],
                              c_im[direction], reverse=(direction == 1))
    y = jax.nn.gelu(y.reshape(B_, S_, D_)).astype(h.dtype)
    a, g = jnp.split(y @ w_glu, 2, axis=-1)
    return a * jax.nn.sigmoid(g)


def setup_inputs(seed: int = 0) -> dict:
    key = jax.random.key(seed)
    ks = iter(jax.random.split(key, 48))
    f32 = jnp.float32
    D = D_MODEL

    def nrm(shape, scale):
        return jax.random.normal(next(ks), shape, f32) * scale

    def gain(shape):
        return 1.0 + nrm(shape, 0.02)

    n0, n1, n2, n3 = [len(range(k, DEPTH, N_MIXERS)) for k in range(N_MIXERS)]
    G, P, Cg = S5_GROUPS, S5_STATE, S5_GC
    x = jax.random.normal(next(ks), (BATCH, SEQ, D), f32)
    mix_norm = gain((DEPTH, D))
    fnet_w = nrm((n0, D, D), D ** -0.5)
    fnet_b = nrm((n0, D), 0.01)
    conv_w_in = nrm((n1, D, 2 * D), D ** -0.5)
    conv_b_in = nrm((n1, 2 * D), 0.01)
    conv_w_dw = nrm((n1, CONV_WIDTH, D), CONV_WIDTH ** -0.5)
    conv_b_dw = nrm((n1, D), 0.01)
    conv_ln_g = gain((n1, D))
    conv_ln_b = nrm((n1, D), 0.01)
    conv_w_out = nrm((n1, D, D), D ** -0.5)
    conv_b_out = nrm((n1, D), 0.01)
    na_w_qkv = nrm((n2, D, 3 * D), D ** -0.5)
    na_q_gain = gain((n2, NA_HEAD_DIM))
    na_k_gain = gain((n2, NA_HEAD_DIM))
    na_rpb = nrm((n2, NA_HEADS, 2 * NA_WIN_H - 1, 2 * NA_WIN_W - 1), 0.1)
    na_w_o = nrm((n2, D, D), D ** -0.5)
    s5_w_in = nrm((n3, D, D), D ** -0.5)
    s5_a_re = -0.5 + nrm((n3, 2, G, P), 0.01)
    s5_a_im = jnp.pi * jnp.arange(P, dtype=f32) + nrm((n3, 2, G, P), 0.01)
    s5_log_dt = jax.random.uniform(next(ks), (n3, 2, G), f32,
                                   math.log(1e-3), math.log(1e-1))
    s5_b_re = nrm((n3, 2, G, P, Cg), (2 * Cg) ** -0.5)
    s5_b_im = nrm((n3, 2, G, P, Cg), (2 * Cg) ** -0.5)
    s5_c_re = nrm((n3, 2, G, Cg, P), P ** -0.5)
    s5_c_im = nrm((n3, 2, G, Cg, P), P ** -0.5)
    s5_d = nrm((n3, G, Cg), 1.0)
    s5_w_glu = nrm((n3, D, 2 * D), D ** -0.5)
    mlp_norm = gain((DEPTH, D))
    mlp_w1 = nrm((DEPTH, D, D_FF), D ** -0.5)
    mlp_w2 = nrm((DEPTH, D_FF, D), D_FF ** -0.5)
    return {"x": x, "mix_norm": mix_norm, "fnet_w": fnet_w, "fnet_b": fnet_b,
            "conv_w_in": conv_w_in, "conv_b_in": conv_b_in, "conv_w_dw": conv_w_dw,
            "conv_b_dw": conv_b_dw, "conv_ln_g": conv_ln_g, "conv_ln_b": conv_ln_b,
            "conv_w_out": conv_w_out, "conv_b_out": conv_b_out,
            "na_w_qkv": na_w_qkv, "na_q_gain": na_q_gain, "na_k_gain": na_k_gain,
            "na_rpb": na_rpb, "na_w_o": na_w_o,
            "s5_w_in": s5_w_in, "s5_a_re": s5_a_re, "s5_a_im": s5_a_im,
            "s5_log_dt": s5_log_dt, "s5_b_re": s5_b_re, "s5_b_im": s5_b_im,
            "s5_c_re": s5_c_re, "s5_c_im": s5_c_im, "s5_d": s5_d, "s5_w_glu": s5_w_glu,
            "mlp_norm": mlp_norm, "mlp_w1": mlp_w1, "mlp_w2": mlp_w2}


def reference(x, mix_norm, fnet_w, fnet_b, conv_w_in, conv_b_in, conv_w_dw, conv_b_dw,
              conv_ln_g, conv_ln_b, conv_w_out, conv_b_out, na_w_qkv, na_q_gain,
              na_k_gain, na_rpb, na_w_o, s5_w_in, s5_a_re, s5_a_im, s5_log_dt,
              s5_b_re, s5_b_im, s5_c_re, s5_c_im, s5_d, s5_w_glu,
              mlp_norm, mlp_w1, mlp_w2):
    for i in range(DEPTH):
        kind = i % N_MIXERS
        j = i // N_MIXERS
        h = _rms_norm(x, mix_norm[i])
        if kind == 0:
            m = _fourier_mixer(h, fnet_w[j], fnet_b[j])
        elif kind == 1:
            m = _conv_module(h, conv_w_in[j], conv_b_in[j], conv_w_dw[j], conv_b_dw[j],
                             conv_ln_g[j], conv_ln_b[j], conv_w_out[j], conv_b_out[j])
        elif kind == 2:
            m = _neighborhood_attention(h, na_w_qkv[j], na_q_gain[j], na_k_gain[j],
                                        na_rpb[j], na_w_o[j])
        else:
            m = _s5_mixer(h, s5_w_in[j], s5_a_re[j], s5_a_im[j], s5_log_dt[j],
                          s5_b_re[j], s5_b_im[j], s5_c_re[j], s5_c_im[j], s5_d[j],
                          s5_w_glu[j])
        x = x + m
        h = _rms_norm(x, mlp_norm[i])
        x = x + jnp.square(jax.nn.relu(h @ mlp_w1[i])) @ mlp_w2[i]
    return x
```

```python
import functools
import math

import numpy as np
import jax
import jax.numpy as jnp
from jax import lax
from jax.experimental import pallas as pl
from jax.experimental.pallas import tpu as pltpu

NORM_EPS = 1e-6
FNET_GROUPS = 8
CONV_WIDTH = 31
GRID_W = 64
NA_HEAD_DIM = 128
NA_WIN_H = 8
NA_WIN_W = 16
NA_QROWS = 8
NA_KROWS = 16
S5_CHUNK = 16
MASK_VALUE = -1e30

V7X_VMEM_LIMIT_BYTES = 56 * 1024 * 1024
BF16 = jnp.bfloat16
F32 = jnp.float32


def _params(*sem):
    return pltpu.CompilerParams(dimension_semantics=sem, vmem_limit_bytes=V7X_VMEM_LIMIT_BYTES)


def _rms(x, g):
    return x * lax.rsqrt(jnp.mean(x * x, axis=-1, keepdims=True) + NORM_EPS) * g


def _proj_kernel(*refs, prologue, glu, has_bias, headnorm, has_res):
    it = iter(refs)
    x_ref = next(it)
    x2_ref = next(it) if prologue == "gelu_skip" else None
    p1_ref = next(it) if prologue != "none" else None
    p2_ref = next(it) if prologue == "ln_silu" else None
    w_ref = next(it)
    wg_ref = next(it) if glu else None
    b_ref = next(it) if has_bias else None
    bg_ref = next(it) if (has_bias and glu) else None
    hg_ref = next(it) if headnorm else None
    r_ref = next(it) if has_res else None
    o_ref = next(it)
    xn_ref = next(it) if prologue != "none" else None

    if prologue != "none":
        @pl.when(pl.program_id(1) == 0)
        def _():
            x = x_ref[...].astype(F32)
            if prologue == "rms":
                y = _rms(x, p1_ref[...])
            elif prologue == "ln_silu":
                mu = jnp.mean(x, axis=-1, keepdims=True)
                xc = x - mu
                y = xc * lax.rsqrt(jnp.mean(xc * xc, axis=-1, keepdims=True) + NORM_EPS)
                y = y * p1_ref[...] + p2_ref[...]
                y = y * jax.nn.sigmoid(y)
            else:
                y = jax.nn.gelu(x + p1_ref[...] * x2_ref[...].astype(F32))
            xn_ref[...] = y.astype(BF16)
        xn = xn_ref[...]
    else:
        xn = x_ref[...].astype(BF16)

    acc = jnp.dot(xn, w_ref[...], preferred_element_type=F32)
    if has_bias:
        acc = acc + b_ref[...]
    if glu:
        gate = jnp.dot(xn, wg_ref[...], preferred_element_type=F32)
        if has_bias:
            gate = gate + bg_ref[...]
        acc = acc * jax.nn.sigmoid(gate)
    if has_res:
        acc = acc + r_ref[...]
    if headnorm:
        for h in range(acc.shape[1] // NA_HEAD_DIM):
            sl = slice(h * NA_HEAD_DIM, (h + 1) * NA_HEAD_DIM)
            o_ref[:, sl] = _rms(acc[:, sl], hg_ref[:, sl]).astype(o_ref.dtype)
    else:
        o_ref[...] = acc.astype(o_ref.dtype)


def _proj(x, w, *, n_out, prologue="none", x2=None, p1=None, p2=None, glu=False, bias=None,
          head_gain=None, res=None, out_dtype=F32, tm=512, tn=512, name="proj"):
    M, K = x.shape
    tm = min(tm, M)
    tn = min(tn, n_out)
    nj = n_out // tn
    row = pl.BlockSpec((tm, K), lambda i, j: (i, 0))
    vec = pl.BlockSpec((1, K), lambda i, j: (0, 0))
    col = pl.BlockSpec((1, tn), lambda i, j: (0, j))
    colg = pl.BlockSpec((1, tn), lambda i, j: (0, j + nj))
    args, specs = [x], [row]
    if prologue == "gelu_skip":
        args.append(x2); specs.append(row)
    if prologue != "none":
        args.append(p1.reshape(1, K).astype(F32)); specs.append(vec)
    if prologue == "ln_silu":
        args.append(p2.reshape(1, K).astype(F32)); specs.append(vec)
    args.append(w); specs.append(pl.BlockSpec((K, tn), lambda i, j: (0, j)))
    if glu:
        args.append(w); specs.append(pl.BlockSpec((K, tn), lambda i, j: (0, j + nj)))
    if bias is not None:
        b2 = bias.reshape(1, -1).astype(F32)
        args.append(b2); specs.append(col)
        if glu:
            args.append(b2); specs.append(colg)
    if head_gain is not None:
        args.append(head_gain.reshape(1, n_out).astype(F32)); specs.append(col)
    if res is not None:
        args.append(res); specs.append(pl.BlockSpec((tm, tn), lambda i, j: (i, j)))
    scratch = [pltpu.VMEM((tm, K), BF16)] if prologue != "none" else []
    kern = functools.partial(_proj_kernel, prologue=prologue, glu=glu, has_bias=bias is not None,
                             headnorm=head_gain is not None, has_res=res is not None)
    return pl.pallas_call(
        kern,
        grid=(M // tm, nj),
        in_specs=specs,
        out_specs=pl.BlockSpec((tm, tn), lambda i, j: (i, j)),
        out_shape=jax.ShapeDtypeStruct((M, n_out), out_dtype),
        scratch_shapes=scratch,
        compiler_params=_params("parallel", "arbitrary"),
        name=name,
    )(*args)


def _mlp_kernel(x_ref, g_ref, w1_ref, w2_ref, o_ref, xn_ref):
    @pl.when(pl.program_id(1) == 0)
    def _():
        x = x_ref[...]
        xn_ref[...] = _rms(x, g_ref[...]).astype(BF16)
        o_ref[...] = x

    h = jnp.dot(xn_ref[...], w1_ref[...], preferred_element_type=F32)
    h = jnp.square(jnp.maximum(h, 0.0)).astype(BF16)
    o_ref[...] += jnp.dot(h, w2_ref[...], preferred_element_type=F32)


def _mlp(x, g, w1, w2, *, tm=512, tf=512):
    M, D = x.shape
    F = w1.shape[1]
    tm = min(tm, M)
    tf = min(tf, F)
    return pl.pallas_call(
        _mlp_kernel,
        grid=(M // tm, F // tf),
        in_specs=[pl.BlockSpec((tm, D), lambda i, f: (i, 0)),
                  pl.BlockSpec((1, D), lambda i, f: (0, 0)),
                  pl.BlockSpec((D, tf), lambda i, f: (0, f)),
                  pl.BlockSpec((tf, D), lambda i, f: (f, 0))],
        out_specs=pl.BlockSpec((tm, D), lambda i, f: (i, 0)),
        out_shape=jax.ShapeDtypeStruct((M, D), F32),
        scratch_shapes=[pltpu.VMEM((tm, D), BF16)],
        compiler_params=_params("parallel", "arbitrary"),
        name="mlp",
    )(x, g.reshape(1, D).astype(F32), w1, w2)


def _dft_tables(n, scale):
    k = jnp.arange(n, dtype=jnp.int32)
    ang = ((k[:, None] * k[None, :]) % n).astype(F32) * (2.0 * math.pi / n)
    return jnp.cos(ang) * scale, jnp.sin(ang) * scale


def _fnet_chan_kernel(x_ref, g_ref, t_ref, a_ref, b_ref):
    gc = t_ref.shape[0]
    xn = _rms(x_ref[...], g_ref[...]).astype(BF16)
    for g in range(xn.shape[1] // gc):
        sl = slice(g * gc, (g + 1) * gc)
        r = jnp.dot(xn[:, sl], t_ref[...], preferred_element_type=F32)
        a_ref[:, sl] = r[:, :gc].astype(BF16)
        b_ref[:, sl] = r[:, gc:].astype(BF16)


def _fnet_chan(x, g, table, *, tm=512):
    M, D = x.shape
    gc = table.shape[0]
    tm = min(tm, M)
    row = pl.BlockSpec((tm, D), lambda i: (i, 0))
    return pl.pallas_call(
        _fnet_chan_kernel,
        grid=(M // tm,),
        in_specs=[row, pl.BlockSpec((1, D), lambda i: (0, 0)),
                  pl.BlockSpec((gc, 2 * gc), lambda i: (0, 0))],
        out_specs=[row, row],
        out_shape=[jax.ShapeDtypeStruct((M, D), BF16)] * 2,
        compiler_params=_params("parallel"),
        name="fnet_chan",
    )(x, g.reshape(1, D).astype(F32), table)


def _fnet_seq_kernel(cs_ref, ss_ref, a_ref, b_ref, z_ref):
    z = jnp.dot(cs_ref[...], a_ref[...], preferred_element_type=F32)
    z = z - jnp.dot(ss_ref[...], b_ref[...], preferred_element_type=F32)
    z_ref[...] = z.astype(z_ref.dtype)


def _fnet_seq(cs, ss, a, b, *, tn=256):
    B, S, D = a.shape
    tn = min(tn, D)
    tab = pl.BlockSpec((S, S), lambda bi, j: (0, 0))
    colblk = pl.BlockSpec((None, S, tn), lambda bi, j: (bi, 0, j))
    return pl.pallas_call(
        _fnet_seq_kernel,
        grid=(B, D // tn),
        in_specs=[tab, tab, colblk, colblk],
        out_specs=colblk,
        out_shape=jax.ShapeDtypeStruct((B, S, D), BF16),
        compiler_params=_params("parallel", "arbitrary"),
        name="fnet_seq",
    )(cs, ss, a, b)


def _fourier_layer(x, norm_g, w, b, B, S):
    M, D = x.shape
    gc = D // FNET_GROUPS
    cc, sc = _dft_tables(gc, gc ** -0.5)
    cs, ss = _dft_tables(S, S ** -0.5)
    table = jnp.concatenate([cc, sc], axis=1).astype(BF16)
    a, bm = _fnet_chan(x, norm_g, table)
    z = _fnet_seq(cs.astype(BF16), ss.astype(BF16), a.reshape(B, S, D), bm.reshape(B, S, D))
    return _proj(z.reshape(M, D), w.astype(BF16), n_out=D, bias=b, res=x, name="fnet_out")


CONV_HALO = 16
CONV_ROWS = 64


def _dwconv_kernel(u_ref, w_ref, b_ref, o_ref, pad_ref):
    S, tc = u_ref.shape
    zeros = jnp.zeros((CONV_HALO, tc), F32)
    pad_ref[0:CONV_HALO, :] = zeros
    pad_ref[CONV_HALO + S:2 * CONV_HALO + S, :] = zeros
    pad_ref[CONV_HALO:CONV_HALO + S, :] = u_ref[...]
    first = CONV_HALO - CONV_WIDTH // 2

    def body(r, carry):
        base = pl.multiple_of(r * CONV_ROWS, CONV_ROWS)
        acc = jnp.broadcast_to(b_ref[...], (CONV_ROWS, tc))
        win = pad_ref[pl.ds(base, CONV_ROWS + 2 * CONV_HALO), :]
        for k in range(CONV_WIDTH):
            acc = acc + win[first + k:first + k + CONV_ROWS, :] * w_ref[k:k + 1, :]
        o_ref[pl.ds(base, CONV_ROWS), :] = acc
        return carry

    lax.fori_loop(0, S // CONV_ROWS, body, 0)


def _dwconv(u, w_dw, b_dw, *, tc=256):
    B, S, D = u.shape
    tc = min(tc, D)
    blk = pl.BlockSpec((None, S, tc), lambda bi, c: (bi, 0, c))
    return pl.pallas_call(
        _dwconv_kernel,
        grid=(B, D // tc),
        in_specs=[blk, pl.BlockSpec((CONV_WIDTH, tc), lambda bi, c: (0, c)),
                  pl.BlockSpec((1, tc), lambda bi, c: (0, c))],
        out_specs=blk,
        out_shape=jax.ShapeDtypeStruct((B, S, D), F32),
        scratch_shapes=[pltpu.VMEM((S + 2 * CONV_HALO, tc), F32)],
        compiler_params=_params("parallel", "parallel"),
        name="dwconv",
    )(u, w_dw.astype(F32), b_dw.reshape(1, D).astype(F32))


def _conv_layer(x, norm_g, w_in, b_in, w_dw, b_dw, ln_g, ln_b, w_out, b_out, B, S):
    M, D = x.shape
    u = _proj(x, w_in.astype(BF16), n_out=D, prologue="rms", p1=norm_g, glu=True, bias=b_in,
              name="conv_in")
    v = _dwconv(u.reshape(B, S, D), w_dw, b_dw).reshape(M, D)
    return _proj(v, w_out.astype(BF16), n_out=D, prologue="ln_silu", p1=ln_g, p2=ln_b,
                 bias=b_out, res=x, name="conv_out")


def _na_key_row_starts(rows):
    nblk = rows // NA_QROWS
    return [int(np.clip(i * NA_QROWS - NA_WIN_H // 2, 0, rows - NA_KROWS)) for i in range(nblk)]


def _na_bias_table(rpb, rows):
    kh = min(NA_WIN_H, rows)
    starts = _na_key_row_starts(rows)
    nblk = len(starts)
    r = (np.arange(nblk) * NA_QROWS)[:, None] + np.arange(NA_QROWS)[None, :]
    kabs = np.asarray(starts)[:, None] + np.arange(NA_KROWS)[None, :]
    r0 = np.clip(r - kh // 2, 0, rows - kh)
    vrow = (kabs[:, None, :] >= r0[:, :, None]) & (kabs[:, None, :] < r0[:, :, None] + kh)
    dr = np.clip(kabs[:, None, :] - r[:, :, None] + NA_WIN_H - 1, 0, 2 * NA_WIN_H - 2)
    qc = np.arange(GRID_W)
    cstart = np.clip(qc - NA_WIN_W // 2, 0, GRID_W - NA_WIN_W)
    kc = np.arange(GRID_W)
    vcol = (kc[None, :] >= cstart[:, None]) & (kc[None, :] < cstart[:, None] + NA_WIN_W)
    dc = np.clip(kc[None, :] - qc[:, None] + NA_WIN_W - 1, 0, 2 * NA_WIN_W - 2)
    dr_b = dr[:, :, None, :, None]
    dc_b = dc[None, None, :, None, :]
    valid = vrow[:, :, None, :, None] & vcol[None, None, :, None, :]
    bias = rpb.astype(F32)[:, dr_b, dc_b]
    bias = jnp.where(jnp.asarray(valid)[None], bias, MASK_VALUE)
    return bias.reshape(rpb.shape[0], nblk, NA_QROWS * GRID_W, NA_KROWS * GRID_W)


def _na_kernel(q_ref, k_ref, v_ref, bias_ref, o_ref, *, key_starts):
    nq = NA_QROWS * GRID_W
    nk = NA_KROWS * GRID_W
    for i, ks in enumerate(key_starts):
        q = q_ref[i * nq:(i + 1) * nq, :]
        k = k_ref[ks * GRID_W:ks * GRID_W + nk, :]
        v = v_ref[ks * GRID_W:ks * GRID_W + nk, :]
        s = lax.dot_general(q, k, (((1,), (1,)), ((), ())), preferred_element_type=F32)
        s = s + bias_ref[i]
        p = jnp.exp(s - jnp.max(s, axis=-1, keepdims=True))
        l = jnp.sum(p, axis=-1, keepdims=True)
        o = jnp.dot(p.astype(BF16), v, preferred_element_type=F32)
        o_ref[i * nq:(i + 1) * nq, :] = (o / l).astype(o_ref.dtype)


def _na_attention(qk, v, bias, B, S, H):
    M = qk.shape[0]
    nblk = bias.shape[1]
    rows = S // GRID_W
    dh = NA_HEAD_DIM
    kern = functools.partial(_na_kernel, key_starts=tuple(_na_key_row_starts(rows)))
    return pl.pallas_call(
        kern,
        grid=(H, B),
        in_specs=[pl.BlockSpec((S, dh), lambda h, b: (b, h)),
                  pl.BlockSpec((S, dh), lambda h, b: (b, H + h)),
                  pl.BlockSpec((S, dh), lambda h, b: (b, h)),
                  pl.BlockSpec((None, nblk) + bias.shape[2:], lambda h, b: (h, 0, 0, 0))],
        out_specs=pl.BlockSpec((S, dh), lambda h, b: (b, h)),
        out_shape=jax.ShapeDtypeStruct((M, H * dh), BF16),
        compiler_params=_params("parallel", "arbitrary"),
        name="na_attn",
    )(qk, qk, v, bias)


def _na_layer(x, norm_g, w_qkv, q_gain, k_gain, rpb, w_o, B, S):
    M, D = x.shape
    H = D // NA_HEAD_DIM
    rows = S // GRID_W
    assert rows % NA_QROWS == 0 and rows >= NA_KROWS
    wb = w_qkv.astype(BF16)
    head_gain = jnp.concatenate([jnp.tile(q_gain.astype(F32) * NA_HEAD_DIM ** -0.5, H),
                                 jnp.tile(k_gain.astype(F32), H)])
    qk = _proj(x, wb[:, :2 * D], n_out=2 * D, prologue="rms", p1=norm_g, head_gain=head_gain,
               out_dtype=BF16, name="na_qk")
    v = _proj(x, wb[:, 2 * D:], n_out=D, prologue="rms", p1=norm_g, out_dtype=BF16, name="na_v")
    o = _na_attention(qk, v, _na_bias_table(rpb, rows), B, S, H)
    return _proj(o, w_o.astype(BF16), n_out=D, res=x, name="na_out")


def _s5_operators(a_re, a_im, log_dt, b_re, b_im, c_re, c_im):
    T = S5_CHUNK
    G, P = a_re.shape[1:]
    Cg = b_re.shape[-1]
    lam_re = jnp.minimum(a_re.astype(F32), -1e-4)
    lam_im = a_im.astype(F32)
    dt = jnp.exp(log_dt.astype(F32))[..., None]
    k = jnp.arange(T + 1, dtype=F32)[:, None, None, None]
    mag = jnp.exp(k * (lam_re * dt)[None])
    pw_r = mag * jnp.cos(k * (lam_im * dt)[None])
    pw_i = mag * jnp.sin(k * (lam_im * dt)[None])
    ab_r, ab_i = pw_r[1], pw_i[1]
    den = jnp.square(lam_re) + jnp.square(lam_im)
    w_r, w_i = ab_r - 1.0, ab_i
    f_r = (w_r * lam_re + w_i * lam_im) / den
    f_i = (w_i * lam_re - w_r * lam_im) / den
    br, bi = b_re.astype(F32), b_im.astype(F32)
    bb_r = f_r[..., None] * br - f_i[..., None] * bi
    bb_i = f_r[..., None] * bi + f_i[..., None] * br
    cr, ci = c_re.astype(F32), c_im.astype(F32)
    ca_r = cr[None] * pw_r[:, :, :, None, :] - ci[None] * pw_i[:, :, :, None, :]
    ca_i = cr[None] * pw_i[:, :, :, None, :] + ci[None] * pw_r[:, :, :, None, :]
    kern = (jnp.einsum("kdgcp,dgpe->kdgce", ca_r[:T], bb_r)
            - jnp.einsum("kdgcp,dgpe->kdgce", ca_i[:T], bb_i))
    s_idx = np.arange(T)[:, None]
    t_idx = np.arange(T)[None, :]
    lag_f = np.clip(t_idx - s_idx, 0, T - 1)
    lag_b = np.clip(s_idx - t_idx, 0, T - 1)
    mf = jnp.where(jnp.asarray(t_idx >= s_idx)[:, :, None, None, None], kern[lag_f, 0], 0.0)
    mb = jnp.where(jnp.asarray(s_idx >= t_idx)[:, :, None, None, None], kern[lag_b, 1], 0.0)
    m = jnp.transpose(mf + mb, (2, 0, 4, 1, 3)).reshape(G, T * Cg, T * Cg)
    pf = np.arange(T)[::-1].copy()
    pb = np.arange(T)

    def state_in(pw, d, order):
        pr, pi = pw_r[order, d], pw_i[order, d]
        e_r = pr[..., None] * bb_r[d][None] - pi[..., None] * bb_i[d][None]
        e_i = pr[..., None] * bb_i[d][None] + pi[..., None] * bb_r[d][None]
        tr = lambda e: jnp.transpose(e, (1, 0, 3, 2)).reshape(G, T * Cg, P)
        return tr(e_r), tr(e_i)

    ef_r, ef_i = state_in(None, 0, pf)
    eb_r, eb_i = state_in(None, 1, pb)
    w1 = jnp.concatenate([m, ef_r, eb_r, ef_i, eb_i], axis=-1)
    of = np.arange(1, T + 1)
    ob = np.arange(T, 0, -1)
    tr = lambda e: jnp.transpose(e, (1, 3, 0, 2)).reshape(G, P, T * Cg)
    w2r = jnp.concatenate([tr(ca_r[of, 0]), tr(ca_r[ob, 1])], axis=1)
    w2i = jnp.concatenate([-tr(ca_i[of, 0]), -tr(ca_i[ob, 1])], axis=1)
    at_r = jnp.concatenate([pw_r[T, 0], pw_r[T, 1]], axis=-1)[:, None, :]
    at_i = jnp.concatenate([pw_i[T, 0], pw_i[T, 1]], axis=-1)[:, None, :]
    return w1.astype(BF16), w2r.astype(BF16), w2i.astype(BF16), at_r, at_i


def _s5_kernel(u_ref, w1_ref, w2r_ref, w2i_ref, ar_ref, ai_ref, y_ref, p_ref, xr_ref, xi_ref, *, nb):
    R, TC = u_ref.shape
    P2 = ar_ref.shape[-1]
    P = P2 // 2
    nchunk = R // nb
    p_ref[...] = jnp.dot(u_ref[...], w1_ref[...], preferred_element_type=F32)
    zeros = jnp.zeros((nb, P), F32)
    last = (nchunk - 1) * nb
    xr_ref[0:nb, 0:P] = zeros
    xi_ref[0:nb, 0:P] = zeros
    xr_ref[last:last + nb, P:P2] = zeros
    xi_ref[last:last + nb, P:P2] = zeros
    ar = jnp.broadcast_to(ar_ref[...], (nb, P2))
    ai = jnp.broadcast_to(ai_ref[...], (nb, P2))
    is_fwd = lax.broadcasted_iota(jnp.int32, (nb, P2), 1) < P

    def body(j, carry):
        xr, xi = carry
        rf = pl.multiple_of(j * nb, nb)
        rb = pl.multiple_of((nchunk - 1 - j) * nb, nb)
        in_r = jnp.where(is_fwd, p_ref[pl.ds(rf, nb), TC:TC + P2], p_ref[pl.ds(rb, nb), TC:TC + P2])
        in_i = jnp.where(is_fwd, p_ref[pl.ds(rf, nb), TC + P2:TC + 2 * P2],
                         p_ref[pl.ds(rb, nb), TC + P2:TC + 2 * P2])
        nxr = ar * xr - ai * xi + in_r
        nxi = ar * xi + ai * xr + in_i
        wf = pl.multiple_of((j + 1) * nb, nb)
        wb = pl.multiple_of((nchunk - 2 - j) * nb, nb)
        xr_ref[pl.ds(wf, nb), 0:P] = nxr[:, 0:P]
        xi_ref[pl.ds(wf, nb), 0:P] = nxi[:, 0:P]
        xr_ref[pl.ds(wb, nb), P:P2] = nxr[:, P:P2]
        xi_ref[pl.ds(wb, nb), P:P2] = nxi[:, P:P2]
        return nxr, nxi

    z = jnp.zeros((nb, P2), F32)
    lax.fori_loop(0, nchunk - 1, body, (z, z))
    y = p_ref[:, 0:TC]
    y = y + jnp.dot(xr_ref[...].astype(BF16), w2r_ref[...], preferred_element_type=F32)
    y = y + jnp.dot(xi_ref[...].astype(BF16), w2i_ref[...], preferred_element_type=F32)
    y_ref[...] = y


def _s5_scan(ug, w1, w2r, w2i, at_r, at_i, nb):
    G, R, TC = ug.shape
    P2 = at_r.shape[-1]
    blk = lambda shape: pl.BlockSpec((None,) + shape, lambda g: (g, 0, 0))
    return pl.pallas_call(
        functools.partial(_s5_kernel, nb=nb),
        grid=(G,),
        in_specs=[blk((R, TC)), blk(w1.shape[1:]), blk(w2r.shape[1:]), blk(w2i.shape[1:]),
                  blk((1, P2)), blk((1, P2))],
        out_specs=blk((R, TC)),
        out_shape=jax.ShapeDtypeStruct((G, R, TC), F32),
        scratch_shapes=[pltpu.VMEM((R, w1.shape[-1]), F32), pltpu.VMEM((R, P2), F32),
                        pltpu.VMEM((R, P2), F32)],
        compiler_params=_params("parallel"),
        name="s5_scan",
    )(ug, w1, w2r, w2i, at_r, at_i)


def _s5_layer(x, norm_g, w_in, a_re, a_im, log_dt, b_re, b_im, c_re, c_im, d_skip, w_glu, B, S):
    M, D = x.shape
    T = S5_CHUNK
    G, Cg = d_skip.shape
    nchunk = S // T
    u = _proj(x, w_in.astype(BF16), n_out=D, prologue="rms", p1=norm_g, name="s5_in")
    w1, w2r, w2i, at_r, at_i = _s5_operators(a_re, a_im, log_dt, b_re, b_im, c_re, c_im)
    ug = u.astype(BF16).reshape(B, nchunk, T, G, Cg).transpose(3, 1, 0, 2, 4).reshape(G, nchunk * B, T * Cg)
    yg = _s5_scan(ug, w1, w2r, w2i, at_r, at_i, B)
    y = yg.reshape(G, nchunk, B, T, Cg).transpose(2, 1, 3, 0, 4).reshape(M, D)
    return _proj(y, w_glu.astype(BF16), n_out=D, prologue="gelu_skip", x2=u, p1=d_skip.reshape(D),
                 glu=True, res=x, name="s5_out")


def kernel(x, mix_norm, fnet_w, fnet_b, conv_w_in, conv_b_in, conv_w_dw, conv_b_dw, conv_ln_g,
           conv_ln_b, conv_w_out, conv_b_out, na_w_qkv, na_q_gain, na_k_gain, na_rpb, na_w_o,
           s5_w_in, s5_a_re, s5_a_im, s5_log_dt, s5_b_re, s5_b_im, s5_c_re, s5_c_im, s5_d,
           s5_w_glu, mlp_norm, mlp_w1, mlp_w2):
    B, S, D = x.shape
    depth = mix_norm.shape[0]
    h = x.reshape(B * S, D)
    for i in range(depth):
        kind, j = i % 4, i // 4
        if kind == 0:
            h = _fourier_layer(h, mix_norm[i], fnet_w[j], fnet_b[j], B, S)
        elif kind == 1:
            h = _conv_layer(h, mix_norm[i], conv_w_in[j], conv_b_in[j], conv_w_dw[j], conv_b_dw[j],
                            conv_ln_g[j], conv_ln_b[j], conv_w_out[j], conv_b_out[j], B, S)
        elif kind == 2:
            h = _na_layer(h, mix_norm[i], na_w_qkv[j], na_q_gain[j], na_k_gain[j], na_rpb[j],
                          na_w_o[j], B, S)
        else:
            h = _s5_layer(h, mix_norm[i], s5_w_in[j], s5_a_re[j], s5_a_im[j], s5_log_dt[j],
                          s5_b_re[j], s5_b_im[j], s5_c_re[j], s5_c_im[j], s5_d[j], s5_w_glu[j], B, S)
        h = _mlp(h, mlp_norm[i], mlp_w1[i].astype(BF16), mlp_w2[i].astype(BF16))
    return h.reshape(B, S, D)
```

```python
import functools
import math

import numpy as np
import jax
import jax.numpy as jnp
from jax import lax
from jax.experimental import pallas as pl
from jax.experimental.pallas import tpu as pltpu

NORM_EPS = 1e-6
FNET_GROUPS = 8
CONV_WIDTH = 31
GRID_W = 64
NA_HEAD_DIM = 128
NA_WIN_H = 8
NA_WIN_W = 16
NA_QROWS = 8
NA_KROWS = 16
S5_CHUNK = 16
MASK_VALUE = -1e30

V7X_VMEM_LIMIT_BYTES = 56 * 1024 * 1024
BF16 = jnp.bfloat16
F32 = jnp.float32


def _params(*sem):
    return pltpu.CompilerParams(dimension_semantics=sem, vmem_limit_bytes=V7X_VMEM_LIMIT_BYTES)


def _rms(x, g):
    return x * lax.rsqrt(jnp.mean(x * x, axis=-1, keepdims=True) + NORM_EPS) * g


def _proj_kernel(*refs, prologue, glu, has_bias, headnorm, has_res, tn):
    it = iter(refs)
    x_ref = next(it)
    p1_ref = next(it) if prologue != "none" else None
    p2_ref = next(it) if prologue == "ln_silu" else None
    w_ref = next(it)
    b_ref = next(it) if has_bias else None
    hg_ref = next(it) if headnorm else None
    r_ref = next(it) if has_res else None
    o_ref = next(it)
    xn_ref = next(it) if prologue != "none" else None

    if prologue != "none":
        x = x_ref[...].astype(F32)
        if prologue == "rms":
            y = _rms(x, p1_ref[...])
        else:
            mu = jnp.mean(x, axis=-1, keepdims=True)
            xc = x - mu
            y = xc * lax.rsqrt(jnp.mean(xc * xc, axis=-1, keepdims=True) + NORM_EPS)
            y = y * p1_ref[...] + p2_ref[...]
            y = y * jax.nn.sigmoid(y)
        xn_ref[...] = y.astype(BF16)
        lhs_ref = xn_ref
    else:
        lhs_ref = x_ref

    n_out = o_ref.shape[1]
    for j in range(n_out // tn):
        sl = slice(j * tn, (j + 1) * tn)
        xn = lhs_ref[...].astype(BF16)
        acc = jnp.dot(xn, w_ref[:, sl], preferred_element_type=F32)
        if has_bias:
            acc = acc + b_ref[:, sl]
        if glu:
            slg = slice(n_out + j * tn, n_out + (j + 1) * tn)
            gate = jnp.dot(xn, w_ref[:, slg], preferred_element_type=F32)
            if has_bias:
                gate = gate + b_ref[:, slg]
            acc = acc * jax.nn.sigmoid(gate)
        if has_res:
            acc = acc + r_ref[:, sl]
        if headnorm:
            for h in range(tn // NA_HEAD_DIM):
                hs = slice(j * tn + h * NA_HEAD_DIM, j * tn + (h + 1) * NA_HEAD_DIM)
                hl = slice(h * NA_HEAD_DIM, (h + 1) * NA_HEAD_DIM)
                o_ref[:, hs] = _rms(acc[:, hl], hg_ref[:, hs]).astype(o_ref.dtype)
        else:
            o_ref[:, sl] = acc.astype(o_ref.dtype)


def _proj(x, w, *, n_out, prologue="none", p1=None, p2=None, glu=False, bias=None,
          head_gain=None, res=None, out_dtype=F32, tm=512, tn=512, name="proj"):
    M, K = x.shape
    tm = min(tm, M)
    tn = min(tn, n_out)
    row = pl.BlockSpec((tm, K), lambda i: (i, 0))
    orow = pl.BlockSpec((tm, n_out), lambda i: (i, 0))
    const = lambda a: pl.BlockSpec(a.shape, lambda i: (0,) * a.ndim, pipeline_mode=pl.Buffered(1))
    args, specs = [x], [row]

    def add_const(a):
        args.append(a); specs.append(const(a))

    if prologue != "none":
        add_const(p1.reshape(1, K).astype(F32))
    if prologue == "ln_silu":
        add_const(p2.reshape(1, K).astype(F32))
    add_const(w)
    if bias is not None:
        add_const(bias.reshape(1, -1).astype(F32))
    if head_gain is not None:
        add_const(head_gain.reshape(1, n_out).astype(F32))
    if res is not None:
        args.append(res); specs.append(orow)
    scratch = [pltpu.VMEM((tm, K), BF16)] if prologue != "none" else []
    kern = functools.partial(_proj_kernel, prologue=prologue, glu=glu, has_bias=bias is not None,
                             headnorm=head_gain is not None, has_res=res is not None, tn=tn)
    return pl.pallas_call(
        kern,
        grid=(M // tm,),
        in_specs=specs,
        out_specs=orow,
        out_shape=jax.ShapeDtypeStruct((M, n_out), out_dtype),
        scratch_shapes=scratch,
        compiler_params=_params("parallel"),
        name=name,
    )(*args)


def _mlp_kernel(x_ref, g_ref, w1_ref, w2_ref, o_ref, xn_ref):
    @pl.when(pl.program_id(1) == 0)
    def _():
        x = x_ref[...]
        xn_ref[...] = _rms(x, g_ref[...]).astype(BF16)
        o_ref[...] = x

    h = jnp.dot(xn_ref[...], w1_ref[...], preferred_element_type=F32)
    h = jnp.square(jnp.maximum(h, 0.0)).astype(BF16)
    o_ref[...] += jnp.dot(h, w2_ref[...], preferred_element_type=F32)


def _mlp(x, g, w1, w2, *, tm=512, tf=512):
    M, D = x.shape
    F = w1.shape[1]
    tm = min(tm, M)
    tf = min(tf, F)
    return pl.pallas_call(
        _mlp_kernel,
        grid=(M // tm, F // tf),
        in_specs=[pl.BlockSpec((tm, D), lambda i, f: (i, 0)),
                  pl.BlockSpec((1, D), lambda i, f: (0, 0)),
                  pl.BlockSpec((D, tf), lambda i, f: (0, f)),
                  pl.BlockSpec((tf, D), lambda i, f: (f, 0))],
        out_specs=pl.BlockSpec((tm, D), lambda i, f: (i, 0)),
        out_shape=jax.ShapeDtypeStruct((M, D), F32),
        scratch_shapes=[pltpu.VMEM((tm, D), BF16)],
        compiler_params=_params("parallel", "arbitrary"),
        name="mlp",
    )(x, g.reshape(1, D).astype(F32), w1, w2)


def _dft_tables(n, scale):
    k = jnp.arange(n, dtype=jnp.int32)
    ang = ((k[:, None] * k[None, :]) % n).astype(F32) * (2.0 * math.pi / n)
    return jnp.cos(ang) * scale, jnp.sin(ang) * scale


def _fnet_chan_kernel(x_ref, g_ref, t_ref, a_ref, b_ref):
    gc = t_ref.shape[0]
    xn = _rms(x_ref[...], g_ref[...]).astype(BF16)
    for g in range(xn.shape[1] // gc):
        sl = slice(g * gc, (g + 1) * gc)
        r = jnp.dot(xn[:, sl], t_ref[...], preferred_element_type=F32)
        a_ref[:, sl] = r[:, :gc].astype(BF16)
        b_ref[:, sl] = r[:, gc:].astype(BF16)


def _fnet_chan(x, g, table, *, tm=512):
    M, D = x.shape
    gc = table.shape[0]
    tm = min(tm, M)
    row = pl.BlockSpec((tm, D), lambda i: (i, 0))
    return pl.pallas_call(
        _fnet_chan_kernel,
        grid=(M // tm,),
        in_specs=[row, pl.BlockSpec((1, D), lambda i: (0, 0)),
                  pl.BlockSpec((gc, 2 * gc), lambda i: (0, 0))],
        out_specs=[row, row],
        out_shape=[jax.ShapeDtypeStruct((M, D), BF16)] * 2,
        compiler_params=_params("parallel"),
        name="fnet_chan",
    )(x, g.reshape(1, D).astype(F32), table)


def _fnet_seq_kernel(cs_ref, ss_ref, a_ref, b_ref, z_ref):
    z = jnp.dot(cs_ref[...], a_ref[...], preferred_element_type=F32)
    z = z - jnp.dot(ss_ref[...], b_ref[...], preferred_element_type=F32)
    z_ref[...] = z.astype(z_ref.dtype)


def _fnet_seq(cs, ss, a, b, *, tn=256):
    B, S, D = a.shape
    tn = min(tn, D)
    tab = pl.BlockSpec((S, S), lambda bi, j: (0, 0))
    colblk = pl.BlockSpec((None, S, tn), lambda bi, j: (bi, 0, j))
    return pl.pallas_call(
        _fnet_seq_kernel,
        grid=(B, D // tn),
        in_specs=[tab, tab, colblk, colblk],
        out_specs=colblk,
        out_shape=jax.ShapeDtypeStruct((B, S, D), BF16),
        compiler_params=_params("parallel", "arbitrary"),
        name="fnet_seq",
    )(cs, ss, a, b)


def _fourier_layer(x, norm_g, w, b, B, S):
    M, D = x.shape
    gc = D // FNET_GROUPS
    cc, sc = _dft_tables(gc, gc ** -0.5)
    cs, ss = _dft_tables(S, S ** -0.5)
    table = jnp.concatenate([cc, sc], axis=1).astype(BF16)
    a, bm = _fnet_chan(x, norm_g, table)
    z = _fnet_seq(cs.astype(BF16), ss.astype(BF16), a.reshape(B, S, D), bm.reshape(B, S, D))
    return _proj(z.reshape(M, D), w.astype(BF16), n_out=D, bias=b, res=x, name="fnet_out")


SUBLANES = 8
LANES = 128
CONV_HALO = 16
CONV_ROWS = 64


def _dwconv_kernel(u_ref, w_ref, b_ref, o_ref, pad_ref):
    S, tc = u_ref.shape
    zeros = jnp.zeros((CONV_HALO, tc), F32)
    pad_ref[0:CONV_HALO, :] = zeros
    pad_ref[CONV_HALO + S:2 * CONV_HALO + S, :] = zeros
    pad_ref[CONV_HALO:CONV_HALO + S, :] = u_ref[...]
    first = CONV_HALO - CONV_WIDTH // 2

    def body(r, carry):
        base = pl.multiple_of(r * CONV_ROWS, CONV_ROWS)
        acc = jnp.broadcast_to(b_ref[...], (CONV_ROWS, tc))
        win = pad_ref[pl.ds(base, CONV_ROWS + 2 * CONV_HALO), :]
        for res in range(SUBLANES):
            shifted = win[res:res + CONV_ROWS + 2 * CONV_HALO - SUBLANES, :]
            for k in range(CONV_WIDTH):
                off = first + k
                if off % SUBLANES == res:
                    al = off - res
                    acc = acc + shifted[al:al + CONV_ROWS, :] * w_ref[k:k + 1, :]
        o_ref[pl.ds(base, CONV_ROWS), :] = acc
        return carry

    lax.fori_loop(0, S // CONV_ROWS, body, 0)


def _dwconv(u, w_dw, b_dw, *, tc=256):
    B, S, D = u.shape
    tc = min(tc, D)
    blk = pl.BlockSpec((None, S, tc), lambda bi, c: (bi, 0, c))
    return pl.pallas_call(
        _dwconv_kernel,
        grid=(B, D // tc),
        in_specs=[blk, pl.BlockSpec((CONV_WIDTH, tc), lambda bi, c: (0, c)),
                  pl.BlockSpec((1, tc), lambda bi, c: (0, c))],
        out_specs=blk,
        out_shape=jax.ShapeDtypeStruct((B, S, D), F32),
        scratch_shapes=[pltpu.VMEM((S + 2 * CONV_HALO, tc), F32)],
        compiler_params=_params("parallel", "parallel"),
        name="dwconv",
    )(u, w_dw.astype(F32), b_dw.reshape(1, D).astype(F32))


def _conv_layer(x, norm_g, w_in, b_in, w_dw, b_dw, ln_g, ln_b, w_out, b_out, B, S):
    M, D = x.shape
    u = _proj(x, w_in.astype(BF16), n_out=D, prologue="rms", p1=norm_g, glu=True, bias=b_in,
              name="conv_in")
    v = _dwconv(u.reshape(B, S, D), w_dw, b_dw).reshape(M, D)
    return _proj(v, w_out.astype(BF16), n_out=D, prologue="ln_silu", p1=ln_g, p2=ln_b,
                 bias=b_out, res=x, name="conv_out")


def _na_key_row_starts(rows):
    nblk = rows // NA_QROWS
    return [int(np.clip(i * NA_QROWS - NA_WIN_H // 2, 0, rows - NA_KROWS)) for i in range(nblk)]


NA_TAB = 18 * GRID_W


def _na_column_bias(rpb):
    H, ndr, _ = rpb.shape
    qc = np.arange(GRID_W)
    kc = np.arange(GRID_W)
    cstart = np.clip(qc - NA_WIN_W // 2, 0, GRID_W - NA_WIN_W)
    vcol = (kc[None, :] >= cstart[:, None]) & (kc[None, :] < cstart[:, None] + NA_WIN_W)
    dc = np.clip(kc[None, :] - qc[:, None] + NA_WIN_W - 1, 0, 2 * NA_WIN_W - 2)
    t = jnp.where(jnp.asarray(vcol)[None, None], rpb.astype(F32)[:, :, dc], MASK_VALUE)
    t = jnp.transpose(t, (0, 2, 1, 3)).reshape(H, GRID_W, ndr * GRID_W)
    pad = lambda n: jnp.full((H, GRID_W, n * GRID_W), MASK_VALUE, F32)
    nblk = NA_TAB // GRID_W
    tab_a = jnp.concatenate([t, pad(nblk - ndr)], axis=-1)
    tab_b = jnp.concatenate([pad(1), t, pad(nblk - ndr - 1)], axis=-1)
    return jnp.concatenate([tab_a, tab_b], axis=-1)


def _na_build_bias(tab_ref, bias_ref, key_starts, rows):
    kh = min(NA_WIN_H, rows)
    pair = 2 * GRID_W
    lane = lax.broadcasted_iota(jnp.int32, (GRID_W, pair), 1)
    masked = jnp.full((GRID_W, pair), MASK_VALUE, F32)
    for i, ks in enumerate(key_starts):
        for qr in range(NA_QROWS):
            r = i * NA_QROWS + qr
            r0 = int(np.clip(r - kh // 2, 0, rows - kh))
            for kp in range(NA_KROWS // 2):
                kabs = ks + 2 * kp
                v0 = r0 <= kabs < r0 + kh
                v1 = r0 <= kabs + 1 < r0 + kh
                dr = kabs - r + NA_WIN_H - 1
                if not (v0 or v1):
                    tile = masked
                else:
                    start = dr * GRID_W if dr % 2 == 0 else NA_TAB + (dr + 1) * GRID_W
                    tile = tab_ref[:, start:start + pair]
                    if not v1:
                        tile = jnp.where(lane < GRID_W, tile, MASK_VALUE)
                    elif not v0:
                        tile = jnp.where(lane >= GRID_W, tile, MASK_VALUE)
                bias_ref[i, qr * GRID_W:(qr + 1) * GRID_W, kp * pair:(kp + 1) * pair] = tile


def _na_kernel(q_ref, k_ref, v_ref, tab_ref, o_ref, bias_ref, *, key_starts, rows):
    @pl.when(pl.program_id(1) == 0)
    def _():
        _na_build_bias(tab_ref, bias_ref, key_starts, rows)

    nq = NA_QROWS * GRID_W
    nk = NA_KROWS * GRID_W
    for i, ks in enumerate(key_starts):
        q = q_ref[i * nq:(i + 1) * nq, :]
        k = k_ref[ks * GRID_W:ks * GRID_W + nk, :]
        v = v_ref[ks * GRID_W:ks * GRID_W + nk, :]
        s = lax.dot_general(q, k, (((1,), (1,)), ((), ())), preferred_element_type=F32)
        s = s + bias_ref[i]
        p = jnp.exp(s - jnp.max(s, axis=-1, keepdims=True))
        l = jnp.sum(p, axis=-1, keepdims=True)
        o = jnp.dot(p.astype(BF16), v, preferred_element_type=F32)
        o_ref[i * nq:(i + 1) * nq, :] = (o / l).astype(o_ref.dtype)


def _na_attention(qk, v, tab, B, S, H):
    M = qk.shape[0]
    rows = S // GRID_W
    dh = NA_HEAD_DIM
    starts = tuple(_na_key_row_starts(rows))
    kern = functools.partial(_na_kernel, key_starts=starts, rows=rows)
    return pl.pallas_call(
        kern,
        grid=(H, B),
        in_specs=[pl.BlockSpec((S, dh), lambda h, b: (b, h)),
                  pl.BlockSpec((S, dh), lambda h, b: (b, H + h)),
                  pl.BlockSpec((S, dh), lambda h, b: (b, h)),
                  pl.BlockSpec((None,) + tab.shape[1:], lambda h, b: (h, 0, 0))],
        out_specs=pl.BlockSpec((S, dh), lambda h, b: (b, h)),
        out_shape=jax.ShapeDtypeStruct((M, H * dh), BF16),
        scratch_shapes=[pltpu.VMEM((len(starts), NA_QROWS * GRID_W, NA_KROWS * GRID_W), F32)],
        compiler_params=_params("arbitrary", "arbitrary"),
        name="na_attn",
    )(qk, qk, v, tab)


def _na_layer(x, norm_g, w_qkv, q_gain, k_gain, rpb, w_o, B, S):
    M, D = x.shape
    H = D // NA_HEAD_DIM
    rows = S // GRID_W
    assert rows % NA_QROWS == 0 and rows >= NA_KROWS
    wb = w_qkv.astype(BF16)
    head_gain = jnp.concatenate([jnp.tile(q_gain.astype(F32) * NA_HEAD_DIM ** -0.5, H),
                                 jnp.tile(k_gain.astype(F32), H)])
    qk = _proj(x, wb[:, :2 * D], n_out=2 * D, prologue="rms", p1=norm_g, head_gain=head_gain,
               out_dtype=BF16, name="na_qk")
    v = _proj(x, wb[:, 2 * D:], n_out=D, prologue="rms", p1=norm_g, out_dtype=BF16, name="na_v")
    o = _na_attention(qk, v, _na_column_bias(rpb), B, S, H)
    return _proj(o, w_o.astype(BF16), n_out=D, res=x, name="na_out")


def _s5_operators(a_re, a_im, log_dt, b_re, b_im, c_re, c_im):
    T = S5_CHUNK
    G, P = a_re.shape[1:]
    Cg = b_re.shape[-1]
    lam_re = jnp.minimum(a_re.astype(F32), -1e-4)
    lam_im = a_im.astype(F32)
    dt = jnp.exp(log_dt.astype(F32))[..., None]
    k = jnp.arange(T + 1, dtype=F32)[:, None, None, None]
    mag = jnp.exp(k * (lam_re * dt)[None])
    pw_r = mag * jnp.cos(k * (lam_im * dt)[None])
    pw_i = mag * jnp.sin(k * (lam_im * dt)[None])
    ab_r, ab_i = pw_r[1], pw_i[1]
    den = jnp.square(lam_re) + jnp.square(lam_im)
    w_r, w_i = ab_r - 1.0, ab_i
    f_r = (w_r * lam_re + w_i * lam_im) / den
    f_i = (w_i * lam_re - w_r * lam_im) / den
    br, bi = b_re.astype(F32), b_im.astype(F32)
    bb_r = f_r[..., None] * br - f_i[..., None] * bi
    bb_i = f_r[..., None] * bi + f_i[..., None] * br
    cr, ci = c_re.astype(F32), c_im.astype(F32)
    ca_r = cr[None] * pw_r[:, :, :, None, :] - ci[None] * pw_i[:, :, :, None, :]
    ca_i = cr[None] * pw_i[:, :, :, None, :] + ci[None] * pw_r[:, :, :, None, :]
    kern = (jnp.einsum("kdgcp,dgpe->kdgce", ca_r[:T], bb_r)
            - jnp.einsum("kdgcp,dgpe->kdgce", ca_i[:T], bb_i))
    s_idx = np.arange(T)[:, None]
    t_idx = np.arange(T)[None, :]
    lag_f = np.clip(t_idx - s_idx, 0, T - 1)
    lag_b = np.clip(s_idx - t_idx, 0, T - 1)
    mf = jnp.where(jnp.asarray(t_idx >= s_idx)[:, :, None, None, None], kern[lag_f, 0], 0.0)
    mb = jnp.where(jnp.asarray(s_idx >= t_idx)[:, :, None, None, None], kern[lag_b, 1], 0.0)
    m = jnp.transpose(mf + mb, (2, 0, 4, 1, 3)).reshape(G, T * Cg, T * Cg)
    pf = np.arange(T)[::-1].copy()
    pb = np.arange(T)

    def state_in(pw, d, order):
        pr, pi = pw_r[order, d], pw_i[order, d]
        e_r = pr[..., None] * bb_r[d][None] - pi[..., None] * bb_i[d][None]
        e_i = pr[..., None] * bb_i[d][None] + pi[..., None] * bb_r[d][None]
        tr = lambda e: jnp.transpose(e, (1, 0, 3, 2)).reshape(G, T * Cg, P)
        return tr(e_r), tr(e_i)

    ef_r, ef_i = state_in(None, 0, pf)
    eb_r, eb_i = state_in(None, 1, pb)
    w1 = jnp.concatenate([m, ef_r, eb_r, ef_i, eb_i], axis=-1)
    of = np.arange(1, T + 1)
    ob = np.arange(T, 0, -1)
    tr = lambda e: jnp.transpose(e, (1, 3, 0, 2)).reshape(G, P, T * Cg)
    w2r = jnp.concatenate([tr(ca_r[of, 0]), tr(ca_r[ob, 1])], axis=1)
    w2i = jnp.concatenate([-tr(ca_i[of, 0]), -tr(ca_i[ob, 1])], axis=1)
    at_r = jnp.concatenate([pw_r[T, 0], pw_r[T, 1]], axis=-1)[:, None, :]
    at_i = jnp.concatenate([pw_i[T, 0], pw_i[T, 1]], axis=-1)[:, None, :]
    return w1.astype(BF16), w2r.astype(BF16), w2i.astype(BF16), at_r, at_i


S5_GROUP_BLOCK = 8
S5_BATCH_BLOCK = 8


def _s5_kernel(u_ref, w1_ref, w2r_ref, w2i_ref, ar_ref, ai_ref, y_ref, p_ref, xr_ref, xi_ref):
    nchunk, nl, nbg, _ = u_ref.shape
    TC = nl * LANES
    ng = w1_ref.shape[0]
    nb = nbg // ng
    R = nchunk * nb
    P2 = ar_ref.shape[-1]
    P = P2 // 2
    zeros = jnp.zeros((nb, P), F32)
    last = (nchunk - 1) * nb
    is_fwd = lax.broadcasted_iota(jnp.int32, (nb, P2), 1) < P
    for gi in range(ng):
        rows_g = pl.ds(gi, nb, stride=ng)
        lhs = jnp.concatenate([u_ref[:, pl.ds(l, 1), rows_g, :].reshape(R, LANES) for l in range(nl)], axis=-1)
        lhs = lhs.astype(BF16)
        p_ref[...] = jnp.dot(lhs, w1_ref[gi], preferred_element_type=F32)
        xr_ref[0:nb, 0:P] = zeros
        xi_ref[0:nb, 0:P] = zeros
        xr_ref[last:last + nb, P:P2] = zeros
        xi_ref[last:last + nb, P:P2] = zeros
        ar = jnp.broadcast_to(ar_ref[gi], (nb, P2))
        ai = jnp.broadcast_to(ai_ref[gi], (nb, P2))

        def body(j, carry):
            xr, xi = carry
            rf = pl.multiple_of(j * nb, nb)
            rb = pl.multiple_of((nchunk - 1 - j) * nb, nb)
            in_r = jnp.where(is_fwd, p_ref[pl.ds(rf, nb), TC:TC + P2], p_ref[pl.ds(rb, nb), TC:TC + P2])
            in_i = jnp.where(is_fwd, p_ref[pl.ds(rf, nb), TC + P2:TC + 2 * P2],
                             p_ref[pl.ds(rb, nb), TC + P2:TC + 2 * P2])
            nxr = ar * xr - ai * xi + in_r
            nxi = ar * xi + ai * xr + in_i
            wf = pl.multiple_of((j + 1) * nb, nb)
            wb = pl.multiple_of((nchunk - 2 - j) * nb, nb)
            xr_ref[pl.ds(wf, nb), 0:P] = nxr[:, 0:P]
            xi_ref[pl.ds(wf, nb), 0:P] = nxi[:, 0:P]
            xr_ref[pl.ds(wb, nb), P:P2] = nxr[:, P:P2]
            xi_ref[pl.ds(wb, nb), P:P2] = nxi[:, P:P2]
            return nxr, nxi

        z = jnp.zeros((nb, P2), F32)
        lax.fori_loop(0, nchunk - 1, body, (z, z))
        y = p_ref[:, 0:TC]
        y = y + jnp.dot(xr_ref[...].astype(BF16), w2r_ref[gi], preferred_element_type=F32)
        y = y + jnp.dot(xi_ref[...].astype(BF16), w2i_ref[gi], preferred_element_type=F32)
        for l in range(nl):
            y_ref[:, pl.ds(l, 1), rows_g, :] = y[:, l * LANES:(l + 1) * LANES].reshape(nchunk, 1, nb, LANES)


def _s5_scan(ug, w1, w2r, w2i, at_r, at_i):
    nchunk, nbb, ngb, nl, nbg, _ = ug.shape
    P2 = at_r.shape[-1]
    ng = w1.shape[0] // ngb
    R = nchunk * (nbg // ng)
    blk = lambda a: pl.BlockSpec((ng,) + a.shape[1:], lambda g, b: (g, 0, 0))
    grp = pl.BlockSpec((nchunk, None, None, nl, nbg, LANES), lambda g, b: (0, b, g, 0, 0, 0))
    return pl.pallas_call(
        _s5_kernel,
        grid=(ngb, nbb),
        in_specs=[grp, blk(w1), blk(w2r), blk(w2i), blk(at_r), blk(at_i)],
        out_specs=grp,
        out_shape=jax.ShapeDtypeStruct(ug.shape, F32),
        scratch_shapes=[pltpu.VMEM((R, w1.shape[-1]), F32), pltpu.VMEM((R, P2), F32),
                        pltpu.VMEM((R, P2), F32)],
        compiler_params=_params("parallel", "arbitrary"),
        name="s5_scan",
    )(ug, w1, w2r, w2i, at_r, at_i)


S5_XPOSE_CHUNKS = 32


def _s5_group_kernel(u_ref, o_ref):
    nc, ngb, nl, ng, _ = o_ref.shape
    TC = nl * LANES
    for ci in range(nc):
        t = u_ref[ci * TC:(ci + 1) * TC, :].T.reshape(ngb, ng, TC)
        for l in range(nl):
            o_ref[ci, :, l] = t[:, :, l * LANES:(l + 1) * LANES]


def _s5_grouped_spec(nc, ngb, ng, nb, TC):
    return pl.BlockSpec((nc, None, ngb, TC // LANES, ng, LANES), lambda b, c: (c, b // nb, 0, 0, b % nb, 0))


def _s5_group(u2, B, nchunk, TC):
    G = u2.shape[1]
    nb = min(S5_BATCH_BLOCK, B)
    ng = min(S5_GROUP_BLOCK, G)
    nc = min(S5_XPOSE_CHUNKS, nchunk)
    steps = nchunk // nc
    return pl.pallas_call(
        _s5_group_kernel,
        grid=(B, steps),
        in_specs=[pl.BlockSpec((nc * TC, G), lambda b, c: (b * steps + c, 0))],
        out_specs=_s5_grouped_spec(nc, G // ng, ng, nb, TC),
        out_shape=jax.ShapeDtypeStruct((nchunk, B // nb, G // ng, TC // LANES, nb * ng, LANES), F32),
        compiler_params=_params("parallel", "parallel"),
        name="s5_group",
    )(u2)


def _s5_ungroup_kernel(y_ref, u_ref, d_ref, o_ref):
    nc, ngb, nl, ng, _ = y_ref.shape
    TC = nl * LANES
    for ci in range(nc):
        rows = slice(ci * TC, (ci + 1) * TC)
        t = jnp.concatenate([y_ref[ci, :, l] for l in range(nl)], axis=-1)
        y = t.reshape(ngb * ng, TC).T + d_ref[...] * u_ref[rows, :]
        o_ref[rows, :] = jax.nn.gelu(y).astype(o_ref.dtype)


def _s5_ungroup(yg, u2, d_tile, B):
    nchunk, nbb, ngb, nl, nbg, _ = yg.shape
    TC = nl * LANES
    G = u2.shape[1]
    nb = B // nbb
    nc = min(S5_XPOSE_CHUNKS, nchunk)
    steps = nchunk // nc
    rows = pl.BlockSpec((nc * TC, G), lambda b, c: (b * steps + c, 0))
    return pl.pallas_call(
        _s5_ungroup_kernel,
        grid=(B, steps),
        in_specs=[_s5_grouped_spec(nc, ngb, nbg // nb, nb, TC), rows,
                  pl.BlockSpec((TC, G), lambda b, c: (0, 0))],
        out_specs=rows,
        out_shape=jax.ShapeDtypeStruct(u2.shape, BF16),
        compiler_params=_params("parallel", "parallel"),
        name="s5_ungroup",
    )(yg, u2, d_tile)


def _s5_layer(x, norm_g, w_in, a_re, a_im, log_dt, b_re, b_im, c_re, c_im, d_skip, w_glu, B, S):
    M, D = x.shape
    T = S5_CHUNK
    G, Cg = d_skip.shape
    nchunk = S // T
    w_in_p = w_in.reshape(D, G, Cg).transpose(0, 2, 1).reshape(D, D).astype(BF16)
    w_glu_p = w_glu.reshape(G, Cg, -1).transpose(1, 0, 2).reshape(D, -1).astype(BF16)
    u = _proj(x, w_in_p, n_out=D, prologue="rms", p1=norm_g, name="s5_in")
    u2 = u.reshape(M * Cg, G)
    w1, w2r, w2i, at_r, at_i = _s5_operators(a_re, a_im, log_dt, b_re, b_im, c_re, c_im)
    ug = _s5_group(u2, B, nchunk, T * Cg)
    yg = _s5_scan(ug, w1, w2r, w2i, at_r, at_i)
    d_tile = jnp.tile(d_skip.astype(F32).T, (T, 1))
    act = _s5_ungroup(yg, u2, d_tile, B).reshape(M, D)
    return _proj(act, w_glu_p, n_out=D, glu=True, res=x, name="s5_out")


def kernel(x, mix_norm, fnet_w, fnet_b, conv_w_in, conv_b_in, conv_w_dw, conv_b_dw, conv_ln_g,
           conv_ln_b, conv_w_out, conv_b_out, na_w_qkv, na_q_gain, na_k_gain, na_rpb, na_w_o,
           s5_w_in, s5_a_re, s5_a_im, s5_log_dt, s5_b_re, s5_b_im, s5_c_re, s5_c_im, s5_d,
           s5_w_glu, mlp_norm, mlp_w1, mlp_w2):
    B, S, D = x.shape
    depth = mix_norm.shape[0]
    h = x.reshape(B * S, D)
    for i in range(depth):
        kind, j = i % 4, i // 4
        if kind == 0:
            h = _fourier_layer(h, mix_norm[i], fnet_w[j], fnet_b[j], B, S)
        elif kind == 1:
            h = _conv_layer(h, mix_norm[i], conv_w_in[j], conv_b_in[j], conv_w_dw[j], conv_b_dw[j],
                            conv_ln_g[j], conv_ln_b[j], conv_w_out[j], conv_b_out[j], B, S)
        elif kind == 2:
            h = _na_layer(h, mix_norm[i], na_w_qkv[j], na_q_gain[j], na_k_gain[j], na_rpb[j],
                          na_w_o[j], B, S)
        else:
            h = _s5_layer(h, mix_norm[i], s5_w_in[j], s5_a_re[j], s5_a_im[j], s5_log_dt[j],
                          s5_b_re[j], s5_b_im[j], s5_c_re[j], s5_c_im[j], s5_d[j], s5_w_glu[j], B, S)
        h = _mlp(h, mlp_norm[i], mlp_w1[i].astype(BF16), mlp_w2[i].astype(BF16))
    return h.reshape(B, S, D)
```

```python
import functools
import math

import numpy as np
import jax
import jax.numpy as jnp
from jax import lax
from jax.experimental import pallas as pl
from jax.experimental.pallas import tpu as pltpu

NORM_EPS = 1e-6
FNET_GROUPS = 8
CONV_WIDTH = 31
GRID_W = 64
NA_HEAD_DIM = 128
NA_WIN_H = 8
NA_WIN_W = 16
NA_QROWS = 8
NA_KROWS = 16
S5_CHUNK = 16
MASK_VALUE = -1e30

V7X_VMEM_LIMIT_BYTES = 56 * 1024 * 1024
BF16 = jnp.bfloat16
F32 = jnp.float32


def _params(*sem):
    return pltpu.CompilerParams(dimension_semantics=sem, vmem_limit_bytes=V7X_VMEM_LIMIT_BYTES)


def _rms(x, g):
    return x * lax.rsqrt(jnp.mean(x * x, axis=-1, keepdims=True) + NORM_EPS) * g


def _proj_kernel(*refs, prologue, glu, has_bias, headnorm, has_res, tn):
    it = iter(refs)
    x_ref = next(it)
    p1_ref = next(it) if prologue != "none" else None
    p2_ref = next(it) if prologue == "ln_silu" else None
    w_ref = next(it)
    b_ref = next(it) if has_bias else None
    hg_ref = next(it) if headnorm else None
    r_ref = next(it) if has_res else None
    o_ref = next(it)
    xn_ref = next(it) if prologue != "none" else None

    if prologue != "none":
        x = x_ref[...].astype(F32)
        if prologue == "rms":
            y = _rms(x, p1_ref[...])
        else:
            mu = jnp.mean(x, axis=-1, keepdims=True)
            xc = x - mu
            y = xc * lax.rsqrt(jnp.mean(xc * xc, axis=-1, keepdims=True) + NORM_EPS)
            y = y * p1_ref[...] + p2_ref[...]
            y = y * jax.nn.sigmoid(y)
        xn_ref[...] = y.astype(BF16)
        lhs_ref = xn_ref
    else:
        lhs_ref = x_ref

    n_out = o_ref.shape[1]
    for j in range(n_out // tn):
        sl = slice(j * tn, (j + 1) * tn)
        xn = lhs_ref[...].astype(BF16)
        acc = jnp.dot(xn, w_ref[:, sl], preferred_element_type=F32)
        if has_bias:
            acc = acc + b_ref[:, sl]
        if glu:
            slg = slice(n_out + j * tn, n_out + (j + 1) * tn)
            gate = jnp.dot(xn, w_ref[:, slg], preferred_element_type=F32)
            if has_bias:
                gate = gate + b_ref[:, slg]
            acc = acc * jax.nn.sigmoid(gate)
        if has_res:
            acc = acc + r_ref[:, sl]
        if headnorm:
            for h in range(tn // NA_HEAD_DIM):
                hs = slice(j * tn + h * NA_HEAD_DIM, j * tn + (h + 1) * NA_HEAD_DIM)
                hl = slice(h * NA_HEAD_DIM, (h + 1) * NA_HEAD_DIM)
                o_ref[:, hs] = _rms(acc[:, hl], hg_ref[:, hs]).astype(o_ref.dtype)
        else:
            o_ref[:, sl] = acc.astype(o_ref.dtype)


def _proj(x, w, *, n_out, prologue="none", p1=None, p2=None, glu=False, bias=None,
          head_gain=None, res=None, out_dtype=F32, tm=512, tn=512, name="proj"):
    M, K = x.shape
    tm = min(tm, M)
    tn = min(tn, n_out)
    row = pl.BlockSpec((tm, K), lambda i: (i, 0))
    orow = pl.BlockSpec((tm, n_out), lambda i: (i, 0))
    const = lambda a: pl.BlockSpec(a.shape, lambda i: (0,) * a.ndim, pipeline_mode=pl.Buffered(1))
    args, specs = [x], [row]

    def add_const(a):
        args.append(a); specs.append(const(a))

    if prologue != "none":
        add_const(p1.reshape(1, K).astype(F32))
    if prologue == "ln_silu":
        add_const(p2.reshape(1, K).astype(F32))
    add_const(w)
    if bias is not None:
        add_const(bias.reshape(1, -1).astype(F32))
    if head_gain is not None:
        add_const(head_gain.reshape(1, n_out).astype(F32))
    if res is not None:
        args.append(res); specs.append(orow)
    scratch = [pltpu.VMEM((tm, K), BF16)] if prologue != "none" else []
    kern = functools.partial(_proj_kernel, prologue=prologue, glu=glu, has_bias=bias is not None,
                             headnorm=head_gain is not None, has_res=res is not None, tn=tn)
    return pl.pallas_call(
        kern,
        grid=(M // tm,),
        in_specs=specs,
        out_specs=orow,
        out_shape=jax.ShapeDtypeStruct((M, n_out), out_dtype),
        scratch_shapes=scratch,
        compiler_params=_params("parallel"),
        name=name,
    )(*args)


def _mlp_kernel(x_ref, g_ref, w1_ref, w2_ref, o_ref, xn_ref):
    @pl.when(pl.program_id(1) == 0)
    def _():
        x = x_ref[...]
        xn_ref[...] = _rms(x, g_ref[...]).astype(BF16)
        o_ref[...] = x

    h = jnp.dot(xn_ref[...], w1_ref[...], preferred_element_type=F32)
    h = jnp.square(jnp.maximum(h, 0.0)).astype(BF16)
    o_ref[...] += jnp.dot(h, w2_ref[...], preferred_element_type=F32)


def _mlp(x, g, w1, w2, layer, *, tm=1024, tf=512):
    M, D = x.shape
    F = w1.shape[2]
    tm = min(tm, M)
    tf = min(tf, F)
    return pl.pallas_call(
        _mlp_kernel,
        grid=(M // tm, F // tf),
        in_specs=[pl.BlockSpec((tm, D), lambda i, f: (i, 0)),
                  pl.BlockSpec((1, D), lambda i, f: (0, 0)),
                  pl.BlockSpec((None, D, tf), lambda i, f: (layer, 0, f)),
                  pl.BlockSpec((None, tf, D), lambda i, f: (layer, f, 0))],
        out_specs=pl.BlockSpec((tm, D), lambda i, f: (i, 0)),
        out_shape=jax.ShapeDtypeStruct((M, D), F32),
        scratch_shapes=[pltpu.VMEM((tm, D), BF16)],
        compiler_params=_params("parallel", "arbitrary"),
        name="mlp",
    )(x, g.reshape(1, D).astype(F32), w1, w2)


def _dft_tables(n, scale):
    k = jnp.arange(n, dtype=jnp.int32)
    ang = ((k[:, None] * k[None, :]) % n).astype(F32) * (2.0 * math.pi / n)
    return jnp.cos(ang) * scale, jnp.sin(ang) * scale


def _fnet_chan_kernel(x_ref, g_ref, t_ref, a_ref, b_ref):
    gc = t_ref.shape[0]
    xn = _rms(x_ref[...], g_ref[...]).astype(BF16)
    for g in range(xn.shape[1] // gc):
        sl = slice(g * gc, (g + 1) * gc)
        r = jnp.dot(xn[:, sl], t_ref[...], preferred_element_type=F32)
        a_ref[:, sl] = r[:, :gc].astype(BF16)
        b_ref[:, sl] = r[:, gc:].astype(BF16)


def _fnet_chan(x, g, table, *, tm=512):
    M, D = x.shape
    gc = table.shape[0]
    tm = min(tm, M)
    row = pl.BlockSpec((tm, D), lambda i: (i, 0))
    return pl.pallas_call(
        _fnet_chan_kernel,
        grid=(M // tm,),
        in_specs=[row, pl.BlockSpec((1, D), lambda i: (0, 0)),
                  pl.BlockSpec((gc, 2 * gc), lambda i: (0, 0))],
        out_specs=[row, row],
        out_shape=[jax.ShapeDtypeStruct((M, D), BF16)] * 2,
        compiler_params=_params("parallel"),
        name="fnet_chan",
    )(x, g.reshape(1, D).astype(F32), table)


def _fnet_seq_kernel(cs_ref, ss_ref, a_ref, b_ref, z_ref):
    z = jnp.dot(cs_ref[...], a_ref[...], preferred_element_type=F32)
    z = z - jnp.dot(ss_ref[...], b_ref[...], preferred_element_type=F32)
    z_ref[...] = z.astype(z_ref.dtype)


def _fnet_seq(cs, ss, a, b, *, tn=256):
    B, S, D = a.shape
    tn = min(tn, D)
    tab = pl.BlockSpec((S, S), lambda bi, j: (0, 0))
    colblk = pl.BlockSpec((None, S, tn), lambda bi, j: (bi, 0, j))
    return pl.pallas_call(
        _fnet_seq_kernel,
        grid=(B, D // tn),
        in_specs=[tab, tab, colblk, colblk],
        out_specs=colblk,
        out_shape=jax.ShapeDtypeStruct((B, S, D), BF16),
        compiler_params=_params("parallel", "arbitrary"),
        name="fnet_seq",
    )(cs, ss, a, b)


def _fourier_layer(x, norm_g, w, b, B, S):
    M, D = x.shape
    gc = D // FNET_GROUPS
    cc, sc = _dft_tables(gc, gc ** -0.5)
    cs, ss = _dft_tables(S, S ** -0.5)
    table = jnp.concatenate([cc, sc], axis=1).astype(BF16)
    a, bm = _fnet_chan(x, norm_g, table)
    z = _fnet_seq(cs.astype(BF16), ss.astype(BF16), a.reshape(B, S, D), bm.reshape(B, S, D))
    return _proj(z.reshape(M, D), w.astype(BF16), n_out=D, bias=b, res=x, name="fnet_out")


SUBLANES = 8
LANES = 128
CONV_HALO = 16
CONV_ROWS = 64


def _dwconv_kernel(u_ref, w_ref, b_ref, o_ref, pad_ref):
    S, tc = u_ref.shape
    zeros = jnp.zeros((CONV_HALO, tc), F32)
    pad_ref[0:CONV_HALO, :] = zeros
    pad_ref[CONV_HALO + S:2 * CONV_HALO + S, :] = zeros
    pad_ref[CONV_HALO:CONV_HALO + S, :] = u_ref[...]
    first = CONV_HALO - CONV_WIDTH // 2

    def body(r, carry):
        base = pl.multiple_of(r * CONV_ROWS, CONV_ROWS)
        acc = jnp.broadcast_to(b_ref[...], (CONV_ROWS, tc))
        for res in range(SUBLANES):
            nrows = CONV_ROWS + (SUBLANES if res else 0)
            part = None
            for k in range(CONV_WIDTH):
                off = first + k
                if off % SUBLANES == res:
                    start = pl.multiple_of(base + (off - res), SUBLANES)
                    term = pad_ref[pl.ds(start, nrows), :] * w_ref[k:k + 1, :]
                    part = term if part is None else part + term
            if part is not None:
                acc = acc + part[res:res + CONV_ROWS, :]
        o_ref[pl.ds(base, CONV_ROWS), :] = acc
        return carry

    lax.fori_loop(0, S // CONV_ROWS, body, 0)


def _dwconv(u, w_dw, b_dw, *, tc=256):
    B, S, D = u.shape
    tc = min(tc, D)
    blk = pl.BlockSpec((None, S, tc), lambda bi, c: (bi, 0, c))
    return pl.pallas_call(
        _dwconv_kernel,
        grid=(B, D // tc),
        in_specs=[blk, pl.BlockSpec((CONV_WIDTH, tc), lambda bi, c: (0, c)),
                  pl.BlockSpec((1, tc), lambda bi, c: (0, c))],
        out_specs=blk,
        out_shape=jax.ShapeDtypeStruct((B, S, D), F32),
        scratch_shapes=[pltpu.VMEM((S + 2 * CONV_HALO, tc), F32)],
        compiler_params=_params("parallel", "parallel"),
        name="dwconv",
    )(u, w_dw.astype(F32), b_dw.reshape(1, D).astype(F32))


def _conv_layer(x, norm_g, w_in, b_in, w_dw, b_dw, ln_g, ln_b, w_out, b_out, B, S):
    M, D = x.shape
    u = _proj(x, w_in.astype(BF16), n_out=D, prologue="rms", p1=norm_g, glu=True, bias=b_in,
              name="conv_in")
    v = _dwconv(u.reshape(B, S, D), w_dw, b_dw).reshape(M, D)
    return _proj(v, w_out.astype(BF16), n_out=D, prologue="ln_silu", p1=ln_g, p2=ln_b,
                 bias=b_out, res=x, name="conv_out")


def _na_key_row_starts(rows):
    nblk = rows // NA_QROWS
    return [int(np.clip(i * NA_QROWS - NA_WIN_H // 2, 0, rows - NA_KROWS)) for i in range(nblk)]


NA_TAB = 18 * GRID_W


def _na_column_bias(rpb):
    H, ndr, _ = rpb.shape
    qc = np.arange(GRID_W)
    kc = np.arange(GRID_W)
    cstart = np.clip(qc - NA_WIN_W // 2, 0, GRID_W - NA_WIN_W)
    vcol = (kc[None, :] >= cstart[:, None]) & (kc[None, :] < cstart[:, None] + NA_WIN_W)
    dc = np.clip(kc[None, :] - qc[:, None] + NA_WIN_W - 1, 0, 2 * NA_WIN_W - 2)
    t = jnp.where(jnp.asarray(vcol)[None, None], rpb.astype(F32)[:, :, dc], MASK_VALUE)
    t = jnp.transpose(t, (0, 2, 1, 3)).reshape(H, GRID_W, ndr * GRID_W)
    pad = lambda n: jnp.full((H, GRID_W, n * GRID_W), MASK_VALUE, F32)
    nblk = NA_TAB // GRID_W
    tab_a = jnp.concatenate([t, pad(nblk - ndr)], axis=-1)
    tab_b = jnp.concatenate([pad(1), t, pad(nblk - ndr - 1)], axis=-1)
    return jnp.concatenate([tab_a, tab_b], axis=-1)


def _na_build_bias(tab_ref, bias_ref, key_starts, rows):
    kh = min(NA_WIN_H, rows)
    pair = 2 * GRID_W
    lane = lax.broadcasted_iota(jnp.int32, (GRID_W, pair), 1)
    masked = jnp.full((GRID_W, pair), MASK_VALUE, F32)
    for i, ks in enumerate(key_starts):
        for qr in range(NA_QROWS):
            r = i * NA_QROWS + qr
            r0 = int(np.clip(r - kh // 2, 0, rows - kh))
            for kp in range(NA_KROWS // 2):
                kabs = ks + 2 * kp
                v0 = r0 <= kabs < r0 + kh
                v1 = r0 <= kabs + 1 < r0 + kh
                dr = kabs - r + NA_WIN_H - 1
                if not (v0 or v1):
                    tile = masked
                else:
                    start = dr * GRID_W if dr % 2 == 0 else NA_TAB + (dr + 1) * GRID_W
                    tile = tab_ref[:, start:start + pair]
                    if not v1:
                        tile = jnp.where(lane < GRID_W, tile, MASK_VALUE)
                    elif not v0:
                        tile = jnp.where(lane >= GRID_W, tile, MASK_VALUE)
                bias_ref[i, qr * GRID_W:(qr + 1) * GRID_W, kp * pair:(kp + 1) * pair] = tile


def _na_kernel(q_ref, k_ref, v_ref, tab_ref, o_ref, bias_ref, *, key_starts, rows):
    @pl.when(pl.program_id(1) == 0)
    def _():
        _na_build_bias(tab_ref, bias_ref, key_starts, rows)

    nq = NA_QROWS * GRID_W
    nk = NA_KROWS * GRID_W
    for i, ks in enumerate(key_starts):
        q = q_ref[i * nq:(i + 1) * nq, :]
        k = k_ref[ks * GRID_W:ks * GRID_W + nk, :]
        v = v_ref[ks * GRID_W:ks * GRID_W + nk, :]
        s = lax.dot_general(q, k, (((1,), (1,)), ((), ())), preferred_element_type=F32)
        s = s + bias_ref[i]
        p = jnp.exp(s - jnp.max(s, axis=-1, keepdims=True))
        l = jnp.sum(p, axis=-1, keepdims=True)
        o = jnp.dot(p.astype(BF16), v, preferred_element_type=F32)
        o_ref[i * nq:(i + 1) * nq, :] = (o / l).astype(o_ref.dtype)


def _na_attention(qk, v, tab, B, S, H):
    M = qk.shape[0]
    rows = S // GRID_W
    dh = NA_HEAD_DIM
    starts = tuple(_na_key_row_starts(rows))
    kern = functools.partial(_na_kernel, key_starts=starts, rows=rows)
    return pl.pallas_call(
        kern,
        grid=(H, B),
        in_specs=[pl.BlockSpec((S, dh), lambda h, b: (b, h)),
                  pl.BlockSpec((S, dh), lambda h, b: (b, H + h)),
                  pl.BlockSpec((S, dh), lambda h, b: (b, h)),
                  pl.BlockSpec((None,) + tab.shape[1:], lambda h, b: (h, 0, 0))],
        out_specs=pl.BlockSpec((S, dh), lambda h, b: (b, h)),
        out_shape=jax.ShapeDtypeStruct((M, H * dh), BF16),
        scratch_shapes=[pltpu.VMEM((len(starts), NA_QROWS * GRID_W, NA_KROWS * GRID_W), F32)],
        compiler_params=_params("arbitrary", "arbitrary"),
        name="na_attn",
    )(qk, qk, v, tab)


def _na_layer(x, norm_g, w_qkv, q_gain, k_gain, rpb, w_o, B, S):
    M, D = x.shape
    H = D // NA_HEAD_DIM
    rows = S // GRID_W
    assert rows % NA_QROWS == 0 and rows >= NA_KROWS
    wb = w_qkv.astype(BF16)
    head_gain = jnp.concatenate([jnp.tile(q_gain.astype(F32) * NA_HEAD_DIM ** -0.5, H),
                                 jnp.tile(k_gain.astype(F32), H)])
    qk = _proj(x, wb[:, :2 * D], n_out=2 * D, prologue="rms", p1=norm_g, head_gain=head_gain,
               out_dtype=BF16, name="na_qk")
    v = _proj(x, wb[:, 2 * D:], n_out=D, prologue="rms", p1=norm_g, out_dtype=BF16, name="na_v")
    o = _na_attention(qk, v, _na_column_bias(rpb), B, S, H)
    return _proj(o, w_o.astype(BF16), n_out=D, res=x, name="na_out")


def _s5_operators(a_re, a_im, log_dt, b_re, b_im, c_re, c_im):
    T = S5_CHUNK
    G, P = a_re.shape[1:]
    Cg = b_re.shape[-1]
    lam_re = jnp.minimum(a_re.astype(F32), -1e-4)
    lam_im = a_im.astype(F32)
    dt = jnp.exp(log_dt.astype(F32))[..., None]
    k = jnp.arange(T + 1, dtype=F32)[:, None, None, None]
    mag = jnp.exp(k * (lam_re * dt)[None])
    pw_r = mag * jnp.cos(k * (lam_im * dt)[None])
    pw_i = mag * jnp.sin(k * (lam_im * dt)[None])
    ab_r, ab_i = pw_r[1], pw_i[1]
    den = jnp.square(lam_re) + jnp.square(lam_im)
    w_r, w_i = ab_r - 1.0, ab_i
    f_r = (w_r * lam_re + w_i * lam_im) / den
    f_i = (w_i * lam_re - w_r * lam_im) / den
    br, bi = b_re.astype(F32), b_im.astype(F32)
    bb_r = f_r[..., None] * br - f_i[..., None] * bi
    bb_i = f_r[..., None] * bi + f_i[..., None] * br
    cr, ci = c_re.astype(F32), c_im.astype(F32)
    ca_r = cr[None] * pw_r[:, :, :, None, :] - ci[None] * pw_i[:, :, :, None, :]
    ca_i = cr[None] * pw_i[:, :, :, None, :] + ci[None] * pw_r[:, :, :, None, :]
    kern = (jnp.einsum("kdgcp,dgpe->dgeck", ca_r[:T], bb_r)
            - jnp.einsum("kdgcp,dgpe->dgeck", ca_i[:T], bb_i))
    kf, kb = kern[0], kern[1]
    lags = jnp.concatenate([jnp.flip(kb[..., 1:], axis=-1), kf[..., :1] + kb[..., :1], kf[..., 1:]], axis=-1)
    m = jnp.stack([lags[..., T - 1 - s:2 * T - 1 - s] for s in range(T)], axis=2)
    m = m.reshape(G, Cg * T, Cg * T)
    pf = np.arange(T)[::-1].copy()
    pb = np.arange(T)

    def state_in(d, order):
        pr, pi = pw_r[order, d], pw_i[order, d]
        e_r = pr[..., None] * bb_r[d][None] - pi[..., None] * bb_i[d][None]
        e_i = pr[..., None] * bb_i[d][None] + pi[..., None] * bb_r[d][None]
        tr = lambda e: jnp.transpose(e, (1, 3, 0, 2)).reshape(G, Cg * T, P)
        return tr(e_r), tr(e_i)

    ef_r, ef_i = state_in(0, pf)
    eb_r, eb_i = state_in(1, pb)
    w1 = jnp.concatenate([m, ef_r, eb_r, ef_i, eb_i], axis=-1)
    of = np.arange(1, T + 1)
    ob = np.arange(T, 0, -1)
    tr = lambda e: jnp.transpose(e, (1, 3, 2, 0)).reshape(G, P, Cg * T)
    w2r = jnp.concatenate([tr(ca_r[of, 0]), tr(ca_r[ob, 1])], axis=1)
    w2i = jnp.concatenate([-tr(ca_i[of, 0]), -tr(ca_i[ob, 1])], axis=1)
    at_r = jnp.concatenate([pw_r[T, 0], pw_r[T, 1]], axis=-1)[:, None, :]
    at_i = jnp.concatenate([pw_i[T, 0], pw_i[T, 1]], axis=-1)[:, None, :]
    return w1.astype(BF16), w2r.astype(BF16), w2i.astype(BF16), at_r, at_i


S5_GROUP_BLOCK = 8
S5_BATCH_BLOCK = 8


def _s5_kernel(u_ref, w1_ref, w2r_ref, w2i_ref, ar_ref, ai_ref, y_ref, p_ref, xr_ref, xi_ref):
    nchunk, nl, nbg, _ = u_ref.shape
    TC = nl * LANES
    ng = w1_ref.shape[0]
    nb = nbg // ng
    R = nchunk * nb
    P2 = ar_ref.shape[-1]
    P = P2 // 2
    zeros = jnp.zeros((nb, P), F32)
    last = (nchunk - 1) * nb
    is_fwd = lax.broadcasted_iota(jnp.int32, (nb, P2), 1) < P
    for gi in range(ng):
        rows_g = pl.ds(gi, nb, stride=ng)
        lhs = jnp.concatenate([u_ref[:, pl.ds(l, 1), rows_g, :].reshape(R, LANES) for l in range(nl)], axis=-1)
        lhs = lhs.astype(BF16)
        p_ref[...] = jnp.dot(lhs, w1_ref[gi], preferred_element_type=F32)
        xr_ref[0:nb, 0:P] = zeros
        xi_ref[0:nb, 0:P] = zeros
        xr_ref[last:last + nb, P:P2] = zeros
        xi_ref[last:last + nb, P:P2] = zeros
        ar = jnp.broadcast_to(ar_ref[gi], (nb, P2))
        ai = jnp.broadcast_to(ai_ref[gi], (nb, P2))

        def body(j, carry):
            xr, xi = carry
            rf = pl.multiple_of(j * nb, nb)
            rb = pl.multiple_of((nchunk - 1 - j) * nb, nb)
            in_r = jnp.where(is_fwd, p_ref[pl.ds(rf, nb), TC:TC + P2], p_ref[pl.ds(rb, nb), TC:TC + P2])
            in_i = jnp.where(is_fwd, p_ref[pl.ds(rf, nb), TC + P2:TC + 2 * P2],
                             p_ref[pl.ds(rb, nb), TC + P2:TC + 2 * P2])
            nxr = ar * xr - ai * xi + in_r
            nxi = ar * xi + ai * xr + in_i
            wf = pl.multiple_of((j + 1) * nb, nb)
            wb = pl.multiple_of((nchunk - 2 - j) * nb, nb)
            xr_ref[pl.ds(wf, nb), 0:P] = nxr[:, 0:P]
            xi_ref[pl.ds(wf, nb), 0:P] = nxi[:, 0:P]
            xr_ref[pl.ds(wb, nb), P:P2] = nxr[:, P:P2]
            xi_ref[pl.ds(wb, nb), P:P2] = nxi[:, P:P2]
            return nxr, nxi

        z = jnp.zeros((nb, P2), F32)
        lax.fori_loop(0, nchunk - 1, body, (z, z))
        y = p_ref[:, 0:TC]
        y = y + jnp.dot(xr_ref[...].astype(BF16), w2r_ref[gi], preferred_element_type=F32)
        y = y + jnp.dot(xi_ref[...].astype(BF16), w2i_ref[gi], preferred_element_type=F32)
        for l in range(nl):
            y_ref[:, pl.ds(l, 1), rows_g, :] = y[:, l * LANES:(l + 1) * LANES].reshape(nchunk, 1, nb, LANES)


def _s5_scan(ug, w1, w2r, w2i, at_r, at_i):
    nchunk, nbb, ngb, nl, nbg, _ = ug.shape
    P2 = at_r.shape[-1]
    ng = w1.shape[0] // ngb
    R = nchunk * (nbg // ng)
    blk = lambda a: pl.BlockSpec((ng,) + a.shape[1:], lambda g, b: (g, 0, 0))
    grp = pl.BlockSpec((nchunk, None, None, nl, nbg, LANES), lambda g, b: (0, b, g, 0, 0, 0))
    return pl.pallas_call(
        _s5_kernel,
        grid=(ngb, nbb),
        in_specs=[grp, blk(w1), blk(w2r), blk(w2i), blk(at_r), blk(at_i)],
        out_specs=grp,
        out_shape=jax.ShapeDtypeStruct(ug.shape, F32),
        scratch_shapes=[pltpu.VMEM((R, w1.shape[-1]), F32), pltpu.VMEM((R, P2), F32),
                        pltpu.VMEM((R, P2), F32)],
        compiler_params=_params("parallel", "arbitrary"),
        name="s5_scan",
    )(ug, w1, w2r, w2i, at_r, at_i)


S5_XPOSE_CHUNKS = 32


def _s5_group_kernel(u_ref, o_ref, *, T):
    nc, ngb, nl, ng, _ = o_ref.shape
    TC = nl * LANES
    G = ngb * ng
    for ci in range(nc):
        rows = slice(ci * T, (ci + 1) * T)
        tile = jnp.concatenate([u_ref[rows, c * G:(c + 1) * G] for c in range(TC // T)], axis=0)
        t = tile.T.reshape(ngb, ng, TC)
        for l in range(nl):
            o_ref[ci, :, l] = t[:, :, l * LANES:(l + 1) * LANES]


def _s5_grouped_spec(nc, ngb, ng, nb, TC):
    return pl.BlockSpec((nc, None, ngb, TC // LANES, ng, LANES), lambda b, c: (c, b // nb, 0, 0, b % nb, 0))


def _s5_group(u, B, nchunk, T, G):
    D = u.shape[1]
    TC = D // G * T
    nb = min(S5_BATCH_BLOCK, B)
    ng = min(S5_GROUP_BLOCK, G)
    nc = min(S5_XPOSE_CHUNKS, nchunk)
    steps = nchunk // nc
    return pl.pallas_call(
        functools.partial(_s5_group_kernel, T=T),
        grid=(B, steps),
        in_specs=[pl.BlockSpec((nc * T, D), lambda b, c: (b * steps + c, 0))],
        out_specs=_s5_grouped_spec(nc, G // ng, ng, nb, TC),
        out_shape=jax.ShapeDtypeStruct((nchunk, B // nb, G // ng, TC // LANES, nb * ng, LANES), F32),
        compiler_params=_params("parallel", "parallel"),
        name="s5_group",
    )(u)


def _s5_ungroup_kernel(y_ref, u_ref, d_ref, o_ref, *, T):
    nc, ngb, nl, ng, _ = y_ref.shape
    TC = nl * LANES
    G = ngb * ng
    for ci in range(nc):
        rows = slice(ci * T, (ci + 1) * T)
        t = jnp.concatenate([y_ref[ci, :, l] for l in range(nl)], axis=-1)
        yt = t.reshape(G, TC).T
        for c in range(TC // T):
            cols = slice(c * G, (c + 1) * G)
            y = yt[c * T:(c + 1) * T, :] + d_ref[c:c + 1, :] * u_ref[rows, cols]
            o_ref[rows, cols] = jax.nn.gelu(y).astype(o_ref.dtype)


def _s5_ungroup(yg, u, d_cg, B, T):
    nchunk, nbb, ngb, nl, nbg, _ = yg.shape
    TC = nl * LANES
    D = u.shape[1]
    nb = B // nbb
    nc = min(S5_XPOSE_CHUNKS, nchunk)
    steps = nchunk // nc
    rows = pl.BlockSpec((nc * T, D), lambda b, c: (b * steps + c, 0))
    return pl.pallas_call(
        functools.partial(_s5_ungroup_kernel, T=T),
        grid=(B, steps),
        in_specs=[_s5_grouped_spec(nc, ngb, nbg // nb, nb, TC), rows,
                  pl.BlockSpec(d_cg.shape, lambda b, c: (0, 0))],
        out_specs=rows,
        out_shape=jax.ShapeDtypeStruct(u.shape, BF16),
        compiler_params=_params("parallel", "parallel"),
        name="s5_ungroup",
    )(yg, u, d_cg)


def _s5_layer(x, norm_g, w_in, a_re, a_im, log_dt, b_re, b_im, c_re, c_im, d_skip, w_glu, B, S):
    M, D = x.shape
    T = S5_CHUNK
    G, Cg = d_skip.shape
    nchunk = S // T
    w_in_p = w_in.reshape(D, G, Cg).transpose(0, 2, 1).reshape(D, D).astype(BF16)
    w_glu_p = w_glu.reshape(G, Cg, -1).transpose(1, 0, 2).reshape(D, -1).astype(BF16)
    u = _proj(x, w_in_p, n_out=D, prologue="rms", p1=norm_g, name="s5_in")
    w1, w2r, w2i, at_r, at_i = _s5_operators(a_re, a_im, log_dt, b_re, b_im, c_re, c_im)
    ug = _s5_group(u, B, nchunk, T, G)
    yg = _s5_scan(ug, w1, w2r, w2i, at_r, at_i)
    act = _s5_ungroup(yg, u, d_skip.astype(F32).T, B, T)
    return _proj(act, w_glu_p, n_out=D, glu=True, res=x, name="s5_out")


def kernel(x, mix_norm, fnet_w, fnet_b, conv_w_in, conv_b_in, conv_w_dw, conv_b_dw, conv_ln_g,
           conv_ln_b, conv_w_out, conv_b_out, na_w_qkv, na_q_gain, na_k_gain, na_rpb, na_w_o,
           s5_w_in, s5_a_re, s5_a_im, s5_log_dt, s5_b_re, s5_b_im, s5_c_re, s5_c_im, s5_d,
           s5_w_glu, mlp_norm, mlp_w1, mlp_w2):
    B, S, D = x.shape
    depth = mix_norm.shape[0]
    h = x.reshape(B * S, D)
    w1_bf, w2_bf = mlp_w1.astype(BF16), mlp_w2.astype(BF16)
    for i in range(depth):
        kind, j = i % 4, i // 4
        if kind == 0:
            h = _fourier_layer(h, mix_norm[i], fnet_w[j], fnet_b[j], B, S)
        elif kind == 1:
            h = _conv_layer(h, mix_norm[i], conv_w_in[j], conv_b_in[j], conv_w_dw[j], conv_b_dw[j],
                            conv_ln_g[j], conv_ln_b[j], conv_w_out[j], conv_b_out[j], B, S)
        elif kind == 2:
            h = _na_layer(h, mix_norm[i], na_w_qkv[j], na_q_gain[j], na_k_gain[j], na_rpb[j],
                          na_w_o[j], B, S)
        else:
            h = _s5_layer(h, mix_norm[i], s5_w_in[j], s5_a_re[j], s5_a_im[j], s5_log_dt[j],
                          s5_b_re[j], s5_b_im[j], s5_c_re[j], s5_c_im[j], s5_d[j], s5_w_glu[j], B, S)
        h = _mlp(h, mlp_norm[i], w1_bf, w2_bf, i)
    return h.reshape(B, S, D)
```

```python
import functools
import math

import numpy as np
import jax
import jax.numpy as jnp
from jax import lax
from jax.experimental import pallas as pl
from jax.experimental.pallas import tpu as pltpu

NORM_EPS = 1e-6
FNET_GROUPS = 8
CONV_WIDTH = 31
GRID_W = 64
NA_HEAD_DIM = 128
NA_WIN_H = 8
NA_WIN_W = 16
NA_QROWS = 4
NA_KROWS = 12
S5_CHUNK = 16
MASK_VALUE = -1e30

V7X_VMEM_LIMIT_BYTES = 56 * 1024 * 1024
BF16 = jnp.bfloat16
F32 = jnp.float32


def _params(*sem):
    return pltpu.CompilerParams(dimension_semantics=sem, vmem_limit_bytes=V7X_VMEM_LIMIT_BYTES)


def _rms(x, g):
    return x * lax.rsqrt(jnp.mean(x * x, axis=-1, keepdims=True) + NORM_EPS) * g


def _proj_kernel(*refs, prologue, glu, has_bias, headnorm, has_res, tn):
    it = iter(refs)
    x_ref = next(it)
    p1_ref = next(it) if prologue != "none" else None
    p2_ref = next(it) if prologue == "ln_silu" else None
    w_ref = next(it)
    b_ref = next(it) if has_bias else None
    hg_ref = next(it) if headnorm else None
    r_ref = next(it) if has_res else None
    o_ref = next(it)
    xn_ref = next(it) if prologue != "none" else None

    if prologue != "none":
        x = x_ref[...].astype(F32)
        if prologue == "rms":
            y = _rms(x, p1_ref[...])
        else:
            mu = jnp.mean(x, axis=-1, keepdims=True)
            xc = x - mu
            y = xc * lax.rsqrt(jnp.mean(xc * xc, axis=-1, keepdims=True) + NORM_EPS)
            y = y * p1_ref[...] + p2_ref[...]
            y = y * jax.nn.sigmoid(y)
        xn_ref[...] = y.astype(BF16)
        lhs_ref = xn_ref
    else:
        lhs_ref = x_ref

    n_out = o_ref.shape[1]
    for j in range(n_out // tn):
        sl = slice(j * tn, (j + 1) * tn)
        xn = lhs_ref[...].astype(BF16)
        acc = jnp.dot(xn, w_ref[:, sl], preferred_element_type=F32)
        if has_bias:
            acc = acc + b_ref[:, sl]
        if glu:
            slg = slice(n_out + j * tn, n_out + (j + 1) * tn)
            gate = jnp.dot(xn, w_ref[:, slg], preferred_element_type=F32)
            if has_bias:
                gate = gate + b_ref[:, slg]
            acc = acc * jax.nn.sigmoid(gate)
        if has_res:
            acc = acc + r_ref[:, sl]
        if headnorm:
            for h in range(tn // NA_HEAD_DIM):
                hs = slice(j * tn + h * NA_HEAD_DIM, j * tn + (h + 1) * NA_HEAD_DIM)
                hl = slice(h * NA_HEAD_DIM, (h + 1) * NA_HEAD_DIM)
                o_ref[:, hs] = _rms(acc[:, hl], hg_ref[:, hs]).astype(o_ref.dtype)
        else:
            o_ref[:, sl] = acc.astype(o_ref.dtype)


def _proj(x, w, *, n_out, prologue="none", p1=None, p2=None, glu=False, bias=None,
          head_gain=None, res=None, out_dtype=F32, tm=512, tn=512, name="proj"):
    M, K = x.shape
    tm = min(tm, M)
    tn = min(tn, n_out)
    row = pl.BlockSpec((tm, K), lambda i: (i, 0))
    orow = pl.BlockSpec((tm, n_out), lambda i: (i, 0))
    const = lambda a: pl.BlockSpec(a.shape, lambda i: (0,) * a.ndim, pipeline_mode=pl.Buffered(1))
    args, specs = [x], [row]

    def add_const(a):
        args.append(a); specs.append(const(a))

    if prologue != "none":
        add_const(p1.reshape(1, K).astype(F32))
    if prologue == "ln_silu":
        add_const(p2.reshape(1, K).astype(F32))
    add_const(w)
    if bias is not None:
        add_const(bias.reshape(1, -1).astype(F32))
    if head_gain is not None:
        add_const(head_gain.reshape(1, n_out).astype(F32))
    if res is not None:
        args.append(res); specs.append(orow)
    scratch = [pltpu.VMEM((tm, K), BF16)] if prologue != "none" else []
    kern = functools.partial(_proj_kernel, prologue=prologue, glu=glu, has_bias=bias is not None,
                             headnorm=head_gain is not None, has_res=res is not None, tn=tn)
    return pl.pallas_call(
        kern,
        grid=(M // tm,),
        in_specs=specs,
        out_specs=orow,
        out_shape=jax.ShapeDtypeStruct((M, n_out), out_dtype),
        scratch_shapes=scratch,
        compiler_params=_params("parallel"),
        name=name,
    )(*args)


def _mlp_kernel(x_ref, g_ref, w1_ref, w2_ref, o_ref, xn_ref):
    @pl.when(pl.program_id(1) == 0)
    def _():
        x = x_ref[...]
        xn_ref[...] = _rms(x, g_ref[...]).astype(BF16)
        o_ref[...] = x

    h = jnp.dot(xn_ref[...], w1_ref[...], preferred_element_type=F32)
    h = jnp.square(jnp.maximum(h, 0.0)).astype(BF16)
    o_ref[...] += jnp.dot(h, w2_ref[...].astype(BF16), preferred_element_type=F32)


def _mlp(x, g, w1, w2, layer, *, tm=1024, tf=512):
    M, D = x.shape
    F = w1.shape[2]
    tm = min(tm, M)
    tf = min(tf, F)
    return pl.pallas_call(
        _mlp_kernel,
        grid=(M // tm, F // tf),
        in_specs=[pl.BlockSpec((tm, D), lambda i, f: (i, 0)),
                  pl.BlockSpec((1, D), lambda i, f: (0, 0)),
                  pl.BlockSpec((None, D, tf), lambda i, f: (layer, 0, f)),
                  pl.BlockSpec((None, tf, D), lambda i, f: (layer, f, 0))],
        out_specs=pl.BlockSpec((tm, D), lambda i, f: (i, 0)),
        out_shape=jax.ShapeDtypeStruct((M, D), F32),
        scratch_shapes=[pltpu.VMEM((tm, D), BF16)],
        compiler_params=_params("parallel", "arbitrary"),
        name="mlp",
    )(x, g.reshape(1, D).astype(F32), w1, w2)


def _dft_tables(n, scale):
    k = jnp.arange(n, dtype=jnp.int32)
    ang = ((k[:, None] * k[None, :]) % n).astype(F32) * (2.0 * math.pi / n)
    return jnp.cos(ang) * scale, jnp.sin(ang) * scale


def _fnet_chan_kernel(x_ref, g_ref, t_ref, a_ref, b_ref):
    gc = t_ref.shape[0]
    xn = _rms(x_ref[...], g_ref[...]).astype(BF16)
    for g in range(xn.shape[1] // gc):
        sl = slice(g * gc, (g + 1) * gc)
        r = jnp.dot(xn[:, sl], t_ref[...], preferred_element_type=F32)
        a_ref[:, sl] = r[:, :gc].astype(BF16)
        b_ref[:, sl] = r[:, gc:].astype(BF16)


def _fnet_chan(x, g, table, *, tm=512):
    M, D = x.shape
    gc = table.shape[0]
    tm = min(tm, M)
    row = pl.BlockSpec((tm, D), lambda i: (i, 0))
    return pl.pallas_call(
        _fnet_chan_kernel,
        grid=(M // tm,),
        in_specs=[row, pl.BlockSpec((1, D), lambda i: (0, 0)),
                  pl.BlockSpec((gc, 2 * gc), lambda i: (0, 0))],
        out_specs=[row, row],
        out_shape=[jax.ShapeDtypeStruct((M, D), BF16)] * 2,
        compiler_params=_params("parallel"),
        name="fnet_chan",
    )(x, g.reshape(1, D).astype(F32), table)


def _fnet_seq_kernel(cs_ref, ss_ref, cmid_ref, rev_ref, a_ref, b_ref, z_ref):
    half = cs_ref.shape[0]
    a = a_ref[...]
    pm = jnp.dot(cs_ref[...], a, preferred_element_type=F32)
    qm = jnp.dot(ss_ref[...], b_ref[...], preferred_element_type=F32)
    z_ref[0:half, :] = (pm - qm).astype(z_ref.dtype)
    upper = jnp.dot(rev_ref[...], (pm + qm).astype(BF16), preferred_element_type=F32)
    mid = jnp.dot(cmid_ref[...], a, preferred_element_type=F32)[0:1, :]
    row = lax.broadcasted_iota(jnp.int32, (half, 1), 0)
    z_ref[half:2 * half, :] = jnp.where(row == 0, mid, upper).astype(z_ref.dtype)


def _fnet_seq(cs, ss, a, b, *, tn=512):
    B, S, D = a.shape
    half = S // 2
    tn = min(tn, D)
    k = jnp.arange(half, dtype=jnp.int32)
    rev = (k[:, None] + k[None, :] == half).astype(BF16)
    const = lambda arr: pl.BlockSpec(arr.shape, lambda bi, j: (0, 0))
    colblk = pl.BlockSpec((None, S, tn), lambda bi, j: (bi, 0, j))
    tabs = [cs[:half], ss[:half], cs[half:half + SUBLANES], rev]
    return pl.pallas_call(
        _fnet_seq_kernel,
        grid=(B, D // tn),
        in_specs=[const(t) for t in tabs] + [colblk, colblk],
        out_specs=colblk,
        out_shape=jax.ShapeDtypeStruct((B, S, D), BF16),
        compiler_params=_params("parallel", "arbitrary"),
        name="fnet_seq",
    )(*tabs, a, b)


def _fourier_layer(x, norm_g, w, b, B, S):
    M, D = x.shape
    gc = D // FNET_GROUPS
    cc, sc = _dft_tables(gc, gc ** -0.5)
    cs, ss = _dft_tables(S, S ** -0.5)
    table = jnp.concatenate([cc, sc], axis=1).astype(BF16)
    a, bm = _fnet_chan(x, norm_g, table)
    z = _fnet_seq(cs.astype(BF16), ss.astype(BF16), a.reshape(B, S, D), bm.reshape(B, S, D))
    return _proj(z.reshape(M, D), w.astype(BF16), n_out=D, bias=b, res=x, name="fnet_out")


SUBLANES = 8
LANES = 128
CONV_HALO = 16
CONV_ROWS = 64


def _dwconv_kernel(u_ref, w_ref, b_ref, o_ref, pad_ref):
    S, tc = u_ref.shape
    zeros = jnp.zeros((CONV_HALO, tc), F32)
    pad_ref[0:CONV_HALO, :] = zeros
    pad_ref[CONV_HALO + S:2 * CONV_HALO + S, :] = zeros
    pad_ref[CONV_HALO:CONV_HALO + S, :] = u_ref[...]
    first = CONV_HALO - CONV_WIDTH // 2

    def body(r, carry):
        base = pl.multiple_of(r * CONV_ROWS, CONV_ROWS)
        acc = jnp.broadcast_to(b_ref[...], (CONV_ROWS, tc))
        for res in range(SUBLANES):
            nrows = CONV_ROWS + (SUBLANES if res else 0)
            part = None
            for k in range(CONV_WIDTH):
                off = first + k
                if off % SUBLANES == res:
                    start = pl.multiple_of(base + (off - res), SUBLANES)
                    term = pad_ref[pl.ds(start, nrows), :] * w_ref[k:k + 1, :]
                    part = term if part is None else part + term
            if part is not None:
                acc = acc + part[res:res + CONV_ROWS, :]
        o_ref[pl.ds(base, CONV_ROWS), :] = acc
        return carry

    lax.fori_loop(0, S // CONV_ROWS, body, 0)


def _dwconv(u, w_dw, b_dw, *, tc=256):
    B, S, D = u.shape
    tc = min(tc, D)
    blk = pl.BlockSpec((None, S, tc), lambda bi, c: (bi, 0, c))
    return pl.pallas_call(
        _dwconv_kernel,
        grid=(B, D // tc),
        in_specs=[blk, pl.BlockSpec((CONV_WIDTH, tc), lambda bi, c: (0, c)),
                  pl.BlockSpec((1, tc), lambda bi, c: (0, c))],
        out_specs=blk,
        out_shape=jax.ShapeDtypeStruct((B, S, D), F32),
        scratch_shapes=[pltpu.VMEM((S + 2 * CONV_HALO, tc), F32)],
        compiler_params=_params("parallel", "parallel"),
        name="dwconv",
    )(u, w_dw.astype(F32), b_dw.reshape(1, D).astype(F32))


def _conv_layer(x, norm_g, w_in, b_in, w_dw, b_dw, ln_g, ln_b, w_out, b_out, B, S):
    M, D = x.shape
    u = _proj(x, w_in.astype(BF16), n_out=D, prologue="rms", p1=norm_g, glu=True, bias=b_in,
              name="conv_in")
    v = _dwconv(u.reshape(B, S, D), w_dw, b_dw).reshape(M, D)
    return _proj(v, w_out.astype(BF16), n_out=D, prologue="ln_silu", p1=ln_g, p2=ln_b,
                 bias=b_out, res=x, name="conv_out")


def _na_key_row_starts(rows):
    nblk = rows // NA_QROWS
    return [int(np.clip(i * NA_QROWS - NA_WIN_H // 2, 0, rows - NA_KROWS)) for i in range(nblk)]


NA_TAB = 18 * GRID_W


def _na_column_bias(rpb):
    H, ndr, _ = rpb.shape
    qc = np.arange(GRID_W)
    kc = np.arange(GRID_W)
    cstart = np.clip(qc - NA_WIN_W // 2, 0, GRID_W - NA_WIN_W)
    vcol = (kc[None, :] >= cstart[:, None]) & (kc[None, :] < cstart[:, None] + NA_WIN_W)
    dc = np.clip(kc[None, :] - qc[:, None] + NA_WIN_W - 1, 0, 2 * NA_WIN_W - 2)
    t = jnp.where(jnp.asarray(vcol)[None, None], rpb.astype(F32)[:, :, dc], MASK_VALUE)
    t = jnp.transpose(t, (0, 2, 1, 3)).reshape(H, GRID_W, ndr * GRID_W)
    pad = lambda n: jnp.full((H, GRID_W, n * GRID_W), MASK_VALUE, F32)
    nblk = NA_TAB // GRID_W
    tab_a = jnp.concatenate([t, pad(nblk - ndr)], axis=-1)
    tab_b = jnp.concatenate([pad(1), t, pad(nblk - ndr - 1)], axis=-1)
    return jnp.concatenate([tab_a, tab_b], axis=-1)


def _na_build_bias(tab_ref, bias_ref, key_starts, rows):
    kh = min(NA_WIN_H, rows)
    pair = 2 * GRID_W
    lane = lax.broadcasted_iota(jnp.int32, (GRID_W, pair), 1)
    masked = jnp.full((GRID_W, pair), MASK_VALUE, F32)
    for i, ks in enumerate(key_starts):
        for qr in range(NA_QROWS):
            r = i * NA_QROWS + qr
            r0 = int(np.clip(r - kh // 2, 0, rows - kh))
            for kp in range(NA_KROWS // 2):
                kabs = ks + 2 * kp
                v0 = r0 <= kabs < r0 + kh
                v1 = r0 <= kabs + 1 < r0 + kh
                dr = kabs - r + NA_WIN_H - 1
                if not (v0 or v1):
                    tile = masked
                else:
                    start = dr * GRID_W if dr % 2 == 0 else NA_TAB + (dr + 1) * GRID_W
                    tile = tab_ref[:, start:start + pair]
                    if not v1:
                        tile = jnp.where(lane < GRID_W, tile, MASK_VALUE)
                    elif not v0:
                        tile = jnp.where(lane >= GRID_W, tile, MASK_VALUE)
                bias_ref[i, qr * GRID_W:(qr + 1) * GRID_W, kp * pair:(kp + 1) * pair] = tile


def _na_kernel(q_ref, k_ref, v_ref, tab_ref, o_ref, bias_ref, *, key_starts, rows):
    @pl.when(pl.program_id(1) == 0)
    def _():
        _na_build_bias(tab_ref, bias_ref, key_starts, rows)

    nq = NA_QROWS * GRID_W
    nk = NA_KROWS * GRID_W
    for i, ks in enumerate(key_starts):
        q = q_ref[i * nq:(i + 1) * nq, :]
        k = k_ref[ks * GRID_W:ks * GRID_W + nk, :]
        v = v_ref[ks * GRID_W:ks * GRID_W + nk, :]
        s = lax.dot_general(q, k, (((1,), (1,)), ((), ())), preferred_element_type=F32)
        s = s + bias_ref[i]
        p = jnp.exp(s - jnp.max(s, axis=-1, keepdims=True))
        l = jnp.sum(p, axis=-1, keepdims=True)
        o = jnp.dot(p.astype(BF16), v, preferred_element_type=F32)
        o_ref[i * nq:(i + 1) * nq, :] = (o / l).astype(o_ref.dtype)


def _na_attention(qk, v, tab, B, S, H):
    M = qk.shape[0]
    rows = S // GRID_W
    dh = NA_HEAD_DIM
    starts = tuple(_na_key_row_starts(rows))
    kern = functools.partial(_na_kernel, key_starts=starts, rows=rows)
    return pl.pallas_call(
        kern,
        grid=(H, B),
        in_specs=[pl.BlockSpec((S, dh), lambda h, b: (b, h)),
                  pl.BlockSpec((S, dh), lambda h, b: (b, H + h)),
                  pl.BlockSpec((S, dh), lambda h, b: (b, h)),
                  pl.BlockSpec((None,) + tab.shape[1:], lambda h, b: (h, 0, 0))],
        out_specs=pl.BlockSpec((S, dh), lambda h, b: (b, h)),
        out_shape=jax.ShapeDtypeStruct((M, H * dh), BF16),
        scratch_shapes=[pltpu.VMEM((len(starts), NA_QROWS * GRID_W, NA_KROWS * GRID_W), F32)],
        compiler_params=_params("arbitrary", "arbitrary"),
        name="na_attn",
    )(qk, qk, v, tab)


def _na_layer(x, norm_g, w_qkv, q_gain, k_gain, rpb, w_o, B, S):
    M, D = x.shape
    H = D // NA_HEAD_DIM
    rows = S // GRID_W
    assert rows % NA_QROWS == 0 and rows >= NA_KROWS
    wb = w_qkv.astype(BF16)
    head_gain = jnp.concatenate([jnp.tile(q_gain.astype(F32) * NA_HEAD_DIM ** -0.5, H),
                                 jnp.tile(k_gain.astype(F32), H)])
    qk = _proj(x, wb[:, :2 * D], n_out=2 * D, prologue="rms", p1=norm_g, head_gain=head_gain,
               out_dtype=BF16, name="na_qk")
    v = _proj(x, wb[:, 2 * D:], n_out=D, prologue="rms", p1=norm_g, out_dtype=BF16, name="na_v")
    o = _na_attention(qk, v, _na_column_bias(rpb), B, S, H)
    return _proj(o, w_o.astype(BF16), n_out=D, res=x, name="na_out")


def _s5_operators(a_re, a_im, log_dt, b_re, b_im, c_re, c_im):
    T = S5_CHUNK
    G, P = a_re.shape[1:]
    Cg = b_re.shape[-1]
    lam_re = jnp.minimum(a_re.astype(F32), -1e-4)
    lam_im = a_im.astype(F32)
    dt = jnp.exp(log_dt.astype(F32))[..., None]
    k = jnp.arange(T + 1, dtype=F32)[:, None, None, None]
    mag = jnp.exp(k * (lam_re * dt)[None])
    pw_r = mag * jnp.cos(k * (lam_im * dt)[None])
    pw_i = mag * jnp.sin(k * (lam_im * dt)[None])
    ab_r, ab_i = pw_r[1], pw_i[1]
    den = jnp.square(lam_re) + jnp.square(lam_im)
    w_r, w_i = ab_r - 1.0, ab_i
    f_r = (w_r * lam_re + w_i * lam_im) / den
    f_i = (w_i * lam_re - w_r * lam_im) / den
    br, bi = b_re.astype(F32), b_im.astype(F32)
    bb_r = f_r[..., None] * br - f_i[..., None] * bi
    bb_i = f_r[..., None] * bi + f_i[..., None] * br
    cr, ci = c_re.astype(F32), c_im.astype(F32)
    ca_r = cr[None] * pw_r[:, :, :, None, :] - ci[None] * pw_i[:, :, :, None, :]
    ca_i = cr[None] * pw_i[:, :, :, None, :] + ci[None] * pw_r[:, :, :, None, :]
    kern = (jnp.einsum("kdgcp,dgpe->dgeck", ca_r[:T], bb_r)
            - jnp.einsum("kdgcp,dgpe->dgeck", ca_i[:T], bb_i))
    kf, kb = kern[0], kern[1]
    lags = jnp.concatenate([jnp.flip(kb[..., 1:], axis=-1), kf[..., :1] + kb[..., :1], kf[..., 1:]], axis=-1)
    m = jnp.stack([lags[..., T - 1 - s:2 * T - 1 - s] for s in range(T)], axis=2)
    m = m.reshape(G, Cg * T, Cg * T)
    pf = np.arange(T)[::-1].copy()
    pb = np.arange(T)

    def state_in(d, order):
        pr, pi = pw_r[order, d], pw_i[order, d]
        e_r = pr[..., None] * bb_r[d][None] - pi[..., None] * bb_i[d][None]
        e_i = pr[..., None] * bb_i[d][None] + pi[..., None] * bb_r[d][None]
        tr = lambda e: jnp.transpose(e, (1, 3, 0, 2)).reshape(G, Cg * T, P)
        return tr(e_r), tr(e_i)

    ef_r, ef_i = state_in(0, pf)
    eb_r, eb_i = state_in(1, pb)
    w1 = jnp.concatenate([m, ef_r, eb_r, ef_i, eb_i], axis=-1)
    of = np.arange(1, T + 1)
    ob = np.arange(T, 0, -1)
    tr = lambda e: jnp.transpose(e, (1, 3, 2, 0)).reshape(G, P, Cg * T)
    w2r = jnp.concatenate([tr(ca_r[of, 0]), tr(ca_r[ob, 1])], axis=1)
    w2i = jnp.concatenate([-tr(ca_i[of, 0]), -tr(ca_i[ob, 1])], axis=1)
    at_r = jnp.concatenate([pw_r[T, 0], pw_r[T, 1]], axis=-1)[:, None, :]
    at_i = jnp.concatenate([pw_i[T, 0], pw_i[T, 1]], axis=-1)[:, None, :]
    return w1.astype(BF16), w2r.astype(BF16), w2i.astype(BF16), at_r, at_i


S5_GROUP_BLOCK = 8
S5_BATCH_BLOCK = 8


def _s5_kernel(u_ref, w1_ref, w2r_ref, w2i_ref, ar_ref, ai_ref, y_ref, p_ref, xr_ref, xi_ref):
    nchunk, nl, nbg, _ = u_ref.shape
    TC = nl * LANES
    ng = w1_ref.shape[0]
    nb = nbg // ng
    R = nchunk * nb
    P2 = ar_ref.shape[-1]
    P = P2 // 2
    zeros = jnp.zeros((nb, P), F32)
    last = (nchunk - 1) * nb
    is_fwd = lax.broadcasted_iota(jnp.int32, (nb, P2), 1) < P
    for gi in range(ng):
        rows_g = pl.ds(gi, nb, stride=ng)
        lhs = jnp.concatenate([u_ref[:, pl.ds(l, 1), rows_g, :].reshape(R, LANES) for l in range(nl)], axis=-1)
        lhs = lhs.astype(BF16)
        p_ref[...] = jnp.dot(lhs, w1_ref[gi], preferred_element_type=F32)
        xr_ref[0:nb, 0:P] = zeros
        xi_ref[0:nb, 0:P] = zeros
        xr_ref[last:last + nb, P:P2] = zeros
        xi_ref[last:last + nb, P:P2] = zeros
        ar = jnp.broadcast_to(ar_ref[gi], (nb, P2))
        ai = jnp.broadcast_to(ai_ref[gi], (nb, P2))

        def body(j, carry):
            xr, xi = carry
            rf = pl.multiple_of(j * nb, nb)
            rb = pl.multiple_of((nchunk - 1 - j) * nb, nb)
            in_r = jnp.where(is_fwd, p_ref[pl.ds(rf, nb), TC:TC + P2], p_ref[pl.ds(rb, nb), TC:TC + P2])
            in_i = jnp.where(is_fwd, p_ref[pl.ds(rf, nb), TC + P2:TC + 2 * P2],
                             p_ref[pl.ds(rb, nb), TC + P2:TC + 2 * P2])
            nxr = ar * xr - ai * xi + in_r
            nxi = ar * xi + ai * xr + in_i
            wf = pl.multiple_of((j + 1) * nb, nb)
            wb = pl.multiple_of((nchunk - 2 - j) * nb, nb)
            xr_ref[pl.ds(wf, nb), 0:P] = nxr[:, 0:P]
            xi_ref[pl.ds(wf, nb), 0:P] = nxi[:, 0:P]
            xr_ref[pl.ds(wb, nb), P:P2] = nxr[:, P:P2]
            xi_ref[pl.ds(wb, nb), P:P2] = nxi[:, P:P2]
            return nxr, nxi

        z = jnp.zeros((nb, P2), F32)
        lax.fori_loop(0, nchunk - 1, body, (z, z))
        y = p_ref[:, 0:TC]
        y = y + jnp.dot(xr_ref[...].astype(BF16), w2r_ref[gi], preferred_element_type=F32)
        y = y + jnp.dot(xi_ref[...].astype(BF16), w2i_ref[gi], preferred_element_type=F32)
        for l in range(nl):
            y_ref[:, pl.ds(l, 1), rows_g, :] = y[:, l * LANES:(l + 1) * LANES].reshape(nchunk, 1, nb, LANES)


def _s5_scan(ug, w1, w2r, w2i, at_r, at_i):
    nchunk, nbb, ngb, nl, nbg, _ = ug.shape
    P2 = at_r.shape[-1]
    ng = w1.shape[0] // ngb
    R = nchunk * (nbg // ng)
    blk = lambda a: pl.BlockSpec((ng,) + a.shape[1:], lambda g, b: (g, 0, 0))
    grp = pl.BlockSpec((nchunk, None, None, nl, nbg, LANES), lambda g, b: (0, b, g, 0, 0, 0))
    return pl.pallas_call(
        _s5_kernel,
        grid=(ngb, nbb),
        in_specs=[grp, blk(w1), blk(w2r), blk(w2i), blk(at_r), blk(at_i)],
        out_specs=grp,
        out_shape=jax.ShapeDtypeStruct(ug.shape, F32),
        scratch_shapes=[pltpu.VMEM((R, w1.shape[-1]), F32), pltpu.VMEM((R, P2), F32),
                        pltpu.VMEM((R, P2), F32)],
        compiler_params=_params("parallel", "arbitrary"),
        name="s5_scan",
    )(ug, w1, w2r, w2i, at_r, at_i)


S5_XPOSE_CHUNKS = 32


def _s5_group_kernel(u_ref, o_ref, *, T):
    nc, ngb, nl, ng, _ = o_ref.shape
    TC = nl * LANES
    G = ngb * ng
    for ci in range(nc):
        rows = slice(ci * T, (ci + 1) * T)
        tile = jnp.concatenate([u_ref[rows, c * G:(c + 1) * G] for c in range(TC // T)], axis=0)
        t = tile.T.reshape(ngb, ng, TC)
        for l in range(nl):
            o_ref[ci, :, l] = t[:, :, l * LANES:(l + 1) * LANES]


def _s5_grouped_spec(nc, ngb, ng, nb, TC):
    return pl.BlockSpec((nc, None, ngb, TC // LANES, ng, LANES), lambda b, c: (c, b // nb, 0, 0, b % nb, 0))


def _s5_group(u, B, nchunk, T, G):
    D = u.shape[1]
    TC = D // G * T
    nb = min(S5_BATCH_BLOCK, B)
    ng = min(S5_GROUP_BLOCK, G)
    nc = min(S5_XPOSE_CHUNKS, nchunk)
    steps = nchunk // nc
    return pl.pallas_call(
        functools.partial(_s5_group_kernel, T=T),
        grid=(B, steps),
        in_specs=[pl.BlockSpec((nc * T, D), lambda b, c: (b * steps + c, 0))],
        out_specs=_s5_grouped_spec(nc, G // ng, ng, nb, TC),
        out_shape=jax.ShapeDtypeStruct((nchunk, B // nb, G // ng, TC // LANES, nb * ng, LANES), F32),
        compiler_params=_params("parallel", "parallel"),
        name="s5_group",
    )(u)


def _s5_ungroup_kernel(y_ref, u_ref, d_ref, o_ref, *, T):
    nc, ngb, nl, ng, _ = y_ref.shape
    TC = nl * LANES
    G = ngb * ng
    for ci in range(nc):
        rows = slice(ci * T, (ci + 1) * T)
        t = jnp.concatenate([y_ref[ci, :, l] for l in range(nl)], axis=-1)
        yt = t.reshape(G, TC).T
        for c in range(TC // T):
            cols = slice(c * G, (c + 1) * G)
            y = yt[c * T:(c + 1) * T, :] + d_ref[c:c + 1, :] * u_ref[rows, cols]
            o_ref[rows, cols] = jax.nn.gelu(y).astype(o_ref.dtype)


def _s5_ungroup(yg, u, d_cg, B, T):
    nchunk, nbb, ngb, nl, nbg, _ = yg.shape
    TC = nl * LANES
    D = u.shape[1]
    nb = B // nbb
    nc = min(S5_XPOSE_CHUNKS, nchunk)
    steps = nchunk // nc
    rows = pl.BlockSpec((nc * T, D), lambda b, c: (b * steps + c, 0))
    return pl.pallas_call(
        functools.partial(_s5_ungroup_kernel, T=T),
        grid=(B, steps),
        in_specs=[_s5_grouped_spec(nc, ngb, nbg // nb, nb, TC), rows,
                  pl.BlockSpec(d_cg.shape, lambda b, c: (0, 0))],
        out_specs=rows,
        out_shape=jax.ShapeDtypeStruct(u.shape, BF16),
        compiler_params=_params("parallel", "parallel"),
        name="s5_ungroup",
    )(yg, u, d_cg)


def _s5_layer(x, norm_g, w_in, a_re, a_im, log_dt, b_re, b_im, c_re, c_im, d_skip, w_glu, B, S):
    M, D = x.shape
    T = S5_CHUNK
    G, Cg = d_skip.shape
    nchunk = S // T
    w_in_p = w_in.reshape(D, G, Cg).transpose(0, 2, 1).reshape(D, D).astype(BF16)
    w_glu_p = w_glu.reshape(G, Cg, -1).transpose(1, 0, 2).reshape(D, -1).astype(BF16)
    u = _proj(x, w_in_p, n_out=D, prologue="rms", p1=norm_g, name="s5_in")
    w1, w2r, w2i, at_r, at_i = _s5_operators(a_re, a_im, log_dt, b_re, b_im, c_re, c_im)
    ug = _s5_group(u, B, nchunk, T, G)
    yg = _s5_scan(ug, w1, w2r, w2i, at_r, at_i)
    act = _s5_ungroup(yg, u, d_skip.astype(F32).T, B, T)
    return _proj(act, w_glu_p, n_out=D, glu=True, res=x, name="s5_out")


def kernel(x, mix_norm, fnet_w, fnet_b, conv_w_in, conv_b_in, conv_w_dw, conv_b_dw, conv_ln_g,
           conv_ln_b, conv_w_out, conv_b_out, na_w_qkv, na_q_gain, na_k_gain, na_rpb, na_w_o,
           s5_w_in, s5_a_re, s5_a_im, s5_log_dt, s5_b_re, s5_b_im, s5_c_re, s5_c_im, s5_d,
           s5_w_glu, mlp_norm, mlp_w1, mlp_w2):
    B, S, D = x.shape
    depth = mix_norm.shape[0]
    h = x.reshape(B * S, D)
    w1_bf = mlp_w1.astype(BF16)
    for i in range(depth):
        kind, j = i % 4, i // 4
        if kind == 0:
            h = _fourier_layer(h, mix_norm[i], fnet_w[j], fnet_b[j], B, S)
        elif kind == 1:
            h = _conv_layer(h, mix_norm[i], conv_w_in[j], conv_b_in[j], conv_w_dw[j], conv_b_dw[j],
                            conv_ln_g[j], conv_ln_b[j], conv_w_out[j], conv_b_out[j], B, S)
        elif kind == 2:
            h = _na_layer(h, mix_norm[i], na_w_qkv[j], na_q_gain[j], na_k_gain[j], na_rpb[j],
                          na_w_o[j], B, S)
        else:
            h = _s5_layer(h, mix_norm[i], s5_w_in[j], s5_a_re[j], s5_a_im[j], s5_log_dt[j],
                          s5_b_re[j], s5_b_im[j], s5_c_re[j], s5_c_im[j], s5_d[j], s5_w_glu[j], B, S)
        h = _mlp(h, mlp_norm[i], w1_bf, mlp_w2, i)
    return h.reshape(B, S, D)
```

```python
import functools
import math

import numpy as np
import jax
import jax.numpy as jnp
from jax import lax
from jax.experimental import pallas as pl
from jax.experimental.pallas import tpu as pltpu

NORM_EPS = 1e-6
FNET_GROUPS = 8
CONV_WIDTH = 31
GRID_W = 64
NA_HEAD_DIM = 128
NA_WIN_H = 8
NA_WIN_W = 16
NA_QROWS = 8
NA_KROWS = 16
S5_CHUNK = 16
MASK_VALUE = -1e30

V7X_VMEM_LIMIT_BYTES = 56 * 1024 * 1024
SUBLANES = 8
LANES = 128
BF16 = jnp.bfloat16
F32 = jnp.float32


def _params(*sem):
    return pltpu.CompilerParams(dimension_semantics=sem, vmem_limit_bytes=V7X_VMEM_LIMIT_BYTES)


def _rms(x, g):
    return x * lax.rsqrt(jnp.mean(x * x, axis=-1, keepdims=True) + NORM_EPS) * g


def _proj_kernel(*refs, prologue, glu, has_bias, headnorm, has_res, tn, s5_chunk):
    it = iter(refs)
    x_ref = next(it)
    yg_ref = next(it) if prologue == "s5_act" else None
    p1_ref = next(it) if prologue != "none" else None
    p2_ref = next(it) if prologue == "ln_silu" else None
    w_ref = next(it)
    b_ref = next(it) if has_bias else None
    hg_ref = next(it) if headnorm else None
    r_ref = next(it) if has_res else None
    o_ref = next(it)
    ug_ref = next(it) if (s5_chunk and prologue != "s5_act") else None
    xn_ref = next(it) if prologue != "none" else None

    if prologue == "s5_act":
        _s5_ungroup_tile(yg_ref, x_ref, p1_ref, xn_ref, s5_chunk)
        lhs_ref = xn_ref
    elif prologue != "none":
        x = x_ref[...].astype(F32)
        if prologue == "rms":
            y = _rms(x, p1_ref[...])
        else:
            mu = jnp.mean(x, axis=-1, keepdims=True)
            xc = x - mu
            y = xc * lax.rsqrt(jnp.mean(xc * xc, axis=-1, keepdims=True) + NORM_EPS)
            y = y * p1_ref[...] + p2_ref[...]
            y = y * jax.nn.sigmoid(y)
        xn_ref[...] = y.astype(BF16)
        lhs_ref = xn_ref
    else:
        lhs_ref = x_ref

    n_out = o_ref.shape[1]
    for j in range(n_out // tn):
        sl = slice(j * tn, (j + 1) * tn)
        xn = lhs_ref[...].astype(BF16)
        acc = jnp.dot(xn, w_ref[:, sl], preferred_element_type=F32)
        if has_bias:
            acc = acc + b_ref[:, sl]
        if glu:
            slg = slice(n_out + j * tn, n_out + (j + 1) * tn)
            gate = jnp.dot(xn, w_ref[:, slg], preferred_element_type=F32)
            if has_bias:
                gate = gate + b_ref[:, slg]
            acc = acc * jax.nn.sigmoid(gate)
        if has_res:
            acc = acc + r_ref[:, sl]
        if headnorm:
            for h in range(tn // NA_HEAD_DIM):
                hs = slice(j * tn + h * NA_HEAD_DIM, j * tn + (h + 1) * NA_HEAD_DIM)
                hl = slice(h * NA_HEAD_DIM, (h + 1) * NA_HEAD_DIM)
                o_ref[:, hs] = _rms(acc[:, hl], hg_ref[:, hs]).astype(o_ref.dtype)
        else:
            o_ref[:, sl] = acc.astype(o_ref.dtype)
    if ug_ref is not None:
        _s5_group_tile(o_ref, ug_ref, s5_chunk)


def _proj(x, w, *, n_out, prologue="none", p1=None, p2=None, glu=False, bias=None,
          head_gain=None, res=None, out_dtype=F32, tm=512, tn=512, name="proj",
          s5_chunk=0, s5_grouped=None, s5_group_out=None):
    M, K = x.shape
    tm = min(tm, M)
    tn = min(tn, n_out)
    row = pl.BlockSpec((tm, K), lambda i: (i, 0))
    orow = pl.BlockSpec((tm, n_out), lambda i: (i, 0))
    const = lambda a: pl.BlockSpec(a.shape, lambda i: (0,) * a.ndim, pipeline_mode=pl.Buffered(1))
    args, specs = [x], [row]

    def add_const(a):
        args.append(a); specs.append(const(a))

    if prologue == "s5_act":
        args.append(s5_grouped[0]); specs.append(s5_grouped[1])
        add_const(p1.astype(F32))
    elif prologue != "none":
        add_const(p1.reshape(1, K).astype(F32))
    if prologue == "ln_silu":
        add_const(p2.reshape(1, K).astype(F32))
    add_const(w)
    if bias is not None:
        add_const(bias.reshape(1, -1).astype(F32))
    if head_gain is not None:
        add_const(head_gain.reshape(1, n_out).astype(F32))
    if res is not None:
        args.append(res); specs.append(orow)
    scratch = [pltpu.VMEM((tm, K), BF16)] if prologue != "none" else []
    kern = functools.partial(_proj_kernel, prologue=prologue, glu=glu, has_bias=bias is not None,
                             headnorm=head_gain is not None, has_res=res is not None, tn=tn,
                             s5_chunk=s5_chunk)
    out_specs, out_shape = orow, jax.ShapeDtypeStruct((M, n_out), out_dtype)
    if s5_group_out is not None:
        out_specs = [orow, s5_group_out[1]]
        out_shape = [out_shape, jax.ShapeDtypeStruct(s5_group_out[0], F32)]
    return pl.pallas_call(
        kern,
        grid=(M // tm,),
        in_specs=specs,
        out_specs=out_specs,
        out_shape=out_shape,
        scratch_shapes=scratch,
        compiler_params=_params("parallel"),
        name=name,
    )(*args)


def _mlp_kernel(x_ref, g_ref, w1_ref, w2_ref, o_ref, xn_ref):
    @pl.when(pl.program_id(1) == 0)
    def _():
        x = x_ref[...]
        xn_ref[...] = _rms(x, g_ref[...]).astype(BF16)
        o_ref[...] = x

    h = jnp.dot(xn_ref[...], w1_ref[...], preferred_element_type=F32)
    h = jnp.square(jnp.maximum(h, 0.0)).astype(BF16)
    o_ref[...] += jnp.dot(h, w2_ref[...], preferred_element_type=F32)


def _mlp(x, g, w1, w2, layer, *, tm=1024, tf=512):
    M, D = x.shape
    F = w1.shape[2]
    tm = min(tm, M)
    tf = min(tf, F)
    return pl.pallas_call(
        _mlp_kernel,
        grid=(M // tm, F // tf),
        in_specs=[pl.BlockSpec((tm, D), lambda i, f: (i, 0)),
                  pl.BlockSpec((1, D), lambda i, f: (0, 0)),
                  pl.BlockSpec((None, D, tf), lambda i, f: (layer, 0, f)),
                  pl.BlockSpec((None, tf, D), lambda i, f: (layer, f, 0))],
        out_specs=pl.BlockSpec((tm, D), lambda i, f: (i, 0)),
        out_shape=jax.ShapeDtypeStruct((M, D), F32),
        scratch_shapes=[pltpu.VMEM((tm, D), BF16)],
        compiler_params=_params("parallel", "arbitrary"),
        name="mlp",
    )(x, g.reshape(1, D).astype(F32), w1, w2)


def _dft_tables(n, scale, rows=None):
    k = jnp.arange(n if rows is None else rows, dtype=jnp.int32)
    s = jnp.arange(n, dtype=jnp.int32)
    ang = ((k[:, None] * s[None, :]) % n).astype(F32) * (2.0 * math.pi / n)
    return jnp.cos(ang) * scale, jnp.sin(ang) * scale


def _fnet_chan_kernel(x_ref, g_ref, t_ref, a_ref, b_ref):
    gc = t_ref.shape[0]
    xn = _rms(x_ref[...], g_ref[...]).astype(BF16)
    for g in range(xn.shape[1] // gc):
        sl = slice(g * gc, (g + 1) * gc)
        r = jnp.dot(xn[:, sl], t_ref[...], preferred_element_type=F32)
        a_ref[:, sl] = r[:, :gc].astype(BF16)
        b_ref[:, sl] = r[:, gc:].astype(BF16)


def _fnet_chan(x, g, table, *, tm=512):
    M, D = x.shape
    gc = table.shape[0]
    tm = min(tm, M)
    row = pl.BlockSpec((tm, D), lambda i: (i, 0))
    return pl.pallas_call(
        _fnet_chan_kernel,
        grid=(M // tm,),
        in_specs=[row, pl.BlockSpec((1, D), lambda i: (0, 0)),
                  pl.BlockSpec((gc, 2 * gc), lambda i: (0, 0))],
        out_specs=[row, row],
        out_shape=[jax.ShapeDtypeStruct((M, D), BF16)] * 2,
        compiler_params=_params("parallel"),
        name="fnet_chan",
    )(x, g.reshape(1, D).astype(F32), table)


def _fnet_seq_kernel(cs_ref, ss_ref, cmid_ref, rev_ref, a_ref, b_ref, z_ref):
    half = cs_ref.shape[0]
    a = a_ref[...]
    pm = jnp.dot(cs_ref[...], a, preferred_element_type=F32)
    qm = jnp.dot(ss_ref[...], b_ref[...], preferred_element_type=F32)
    z_ref[0:half, :] = (pm - qm).astype(z_ref.dtype)
    upper = jnp.dot(rev_ref[...], (pm + qm).astype(BF16), preferred_element_type=F32)
    mid = jnp.dot(cmid_ref[...], a, preferred_element_type=F32)[0:1, :]
    row = lax.broadcasted_iota(jnp.int32, (half, 1), 0)
    z_ref[half:2 * half, :] = jnp.where(row == 0, mid, upper).astype(z_ref.dtype)


def _fnet_seq(cs, ss, a, b, *, tn=512):
    B, S, D = a.shape
    half = S // 2
    tn = min(tn, D)
    k = jnp.arange(half, dtype=jnp.int32)
    rev = (k[:, None] + k[None, :] == half).astype(BF16)
    const = lambda arr: pl.BlockSpec(arr.shape, lambda bi, j: (0, 0))
    colblk = pl.BlockSpec((None, S, tn), lambda bi, j: (bi, 0, j))
    tabs = [cs[:half], ss[:half], cs[half:half + SUBLANES], rev]
    return pl.pallas_call(
        _fnet_seq_kernel,
        grid=(B, D // tn),
        in_specs=[const(t) for t in tabs] + [colblk, colblk],
        out_specs=colblk,
        out_shape=jax.ShapeDtypeStruct((B, S, D), BF16),
        compiler_params=_params("parallel", "arbitrary"),
        name="fnet_seq",
    )(*tabs, a, b)


def _fourier_layer(x, norm_g, w, b, B, S):
    M, D = x.shape
    gc = D // FNET_GROUPS
    cc, sc = _dft_tables(gc, gc ** -0.5)
    cs, ss = _dft_tables(S, S ** -0.5, rows=S // 2 + SUBLANES)
    table = jnp.concatenate([cc, sc], axis=1).astype(BF16)
    a, bm = _fnet_chan(x, norm_g, table)
    z = _fnet_seq(cs.astype(BF16), ss.astype(BF16), a.reshape(B, S, D), bm.reshape(B, S, D))
    return _proj(z.reshape(M, D), w.astype(BF16), n_out=D, bias=b, res=x, name="fnet_out")


CONV_HALO = 16
CONV_ROWS = 64


def _dwconv_kernel(u_ref, w_ref, b_ref, o_ref, pad_ref):
    S, tc = u_ref.shape
    zeros = jnp.zeros((CONV_HALO, tc), F32)
    pad_ref[0:CONV_HALO, :] = zeros
    pad_ref[CONV_HALO + S:2 * CONV_HALO + S, :] = zeros
    pad_ref[CONV_HALO:CONV_HALO + S, :] = u_ref[...]
    first = CONV_HALO - CONV_WIDTH // 2

    def body(r, carry):
        base = pl.multiple_of(r * CONV_ROWS, CONV_ROWS)
        acc = jnp.broadcast_to(b_ref[...], (CONV_ROWS, tc))
        for res in range(SUBLANES):
            nrows = CONV_ROWS + (SUBLANES if res else 0)
            part = None
            for k in range(CONV_WIDTH):
                off = first + k
                if off % SUBLANES == res:
                    start = pl.multiple_of(base + (off - res), SUBLANES)
                    term = pad_ref[pl.ds(start, nrows), :] * w_ref[k:k + 1, :]
                    part = term if part is None else part + term
            if part is not None:
                acc = acc + part[res:res + CONV_ROWS, :]
        o_ref[pl.ds(base, CONV_ROWS), :] = acc
        return carry

    lax.fori_loop(0, S // CONV_ROWS, body, 0)


def _dwconv(u, w_dw, b_dw, *, tc=256):
    B, S, D = u.shape
    tc = min(tc, D)
    blk = pl.BlockSpec((None, S, tc), lambda bi, c: (bi, 0, c))
    return pl.pallas_call(
        _dwconv_kernel,
        grid=(B, D // tc),
        in_specs=[blk, pl.BlockSpec((CONV_WIDTH, tc), lambda bi, c: (0, c)),
                  pl.BlockSpec((1, tc), lambda bi, c: (0, c))],
        out_specs=blk,
        out_shape=jax.ShapeDtypeStruct((B, S, D), F32),
        scratch_shapes=[pltpu.VMEM((S + 2 * CONV_HALO, tc), F32)],
        compiler_params=_params("parallel", "parallel"),
        name="dwconv",
    )(u, w_dw.astype(F32), b_dw.reshape(1, D).astype(F32))


def _conv_layer(x, norm_g, w_in, b_in, w_dw, b_dw, ln_g, ln_b, w_out, b_out, B, S):
    M, D = x.shape
    u = _proj(x, w_in.astype(BF16), n_out=D, prologue="rms", p1=norm_g, glu=True, bias=b_in,
              name="conv_in")
    v = _dwconv(u.reshape(B, S, D), w_dw, b_dw).reshape(M, D)
    return _proj(v, w_out.astype(BF16), n_out=D, prologue="ln_silu", p1=ln_g, p2=ln_b,
                 bias=b_out, res=x, name="conv_out")


def _na_key_row_starts(rows):
    nblk = rows // NA_QROWS
    return [int(np.clip(i * NA_QROWS - NA_WIN_H // 2, 0, rows - NA_KROWS)) for i in range(nblk)]


NA_TAB = 18 * GRID_W


def _na_column_bias(rpb):
    H, ndr, _ = rpb.shape
    qc = np.arange(GRID_W)
    kc = np.arange(GRID_W)
    cstart = np.clip(qc - NA_WIN_W // 2, 0, GRID_W - NA_WIN_W)
    vcol = (kc[None, :] >= cstart[:, None]) & (kc[None, :] < cstart[:, None] + NA_WIN_W)
    dc = np.clip(kc[None, :] - qc[:, None] + NA_WIN_W - 1, 0, 2 * NA_WIN_W - 2)
    t = jnp.where(jnp.asarray(vcol)[None, None], rpb.astype(F32)[:, :, dc], MASK_VALUE)
    t = jnp.transpose(t, (0, 2, 1, 3)).reshape(H, GRID_W, ndr * GRID_W)
    pad = lambda n: jnp.full((H, GRID_W, n * GRID_W), MASK_VALUE, F32)
    nblk = NA_TAB // GRID_W
    tab_a = jnp.concatenate([t, pad(nblk - ndr)], axis=-1)
    tab_b = jnp.concatenate([pad(1), t, pad(nblk - ndr - 1)], axis=-1)
    return jnp.concatenate([tab_a, tab_b], axis=-1)


def _na_build_bias(tab_ref, bias_ref, key_starts, rows):
    kh = min(NA_WIN_H, rows)
    pair = 2 * GRID_W
    lane = lax.broadcasted_iota(jnp.int32, (GRID_W, pair), 1)
    masked = jnp.full((GRID_W, pair), MASK_VALUE, F32)
    for i, ks in enumerate(key_starts):
        for qr in range(NA_QROWS):
            r = i * NA_QROWS + qr
            r0 = int(np.clip(r - kh // 2, 0, rows - kh))
            for kp in range(NA_KROWS // 2):
                kabs = ks + 2 * kp
                v0 = r0 <= kabs < r0 + kh
                v1 = r0 <= kabs + 1 < r0 + kh
                dr = kabs - r + NA_WIN_H - 1
                if not (v0 or v1):
                    tile = masked
                else:
                    start = dr * GRID_W if dr % 2 == 0 else NA_TAB + (dr + 1) * GRID_W
                    tile = tab_ref[:, start:start + pair]
                    if not v1:
                        tile = jnp.where(lane < GRID_W, tile, MASK_VALUE)
                    elif not v0:
                        tile = jnp.where(lane >= GRID_W, tile, MASK_VALUE)
                bias_ref[i, qr * GRID_W:(qr + 1) * GRID_W, kp * pair:(kp + 1) * pair] = tile


def _na_kernel(q_ref, k_ref, v_ref, tab_ref, o_ref, bias_ref, *, key_starts, rows):
    @pl.when(pl.program_id(1) == 0)
    def _():
        _na_build_bias(tab_ref, bias_ref, key_starts, rows)

    nq = NA_QROWS * GRID_W
    nk = NA_KROWS * GRID_W
    for i, ks in enumerate(key_starts):
        q = q_ref[i * nq:(i + 1) * nq, :]
        k = k_ref[ks * GRID_W:ks * GRID_W + nk, :]
        v = v_ref[ks * GRID_W:ks * GRID_W + nk, :]
        s = lax.dot_general(q, k, (((1,), (1,)), ((), ())), preferred_element_type=F32)
        s = s + bias_ref[i]
        p = jnp.exp(s - jnp.max(s, axis=-1, keepdims=True))
        l = jnp.sum(p, axis=-1, keepdims=True)
        o = jnp.dot(p.astype(BF16), v, preferred_element_type=F32)
        o_ref[i * nq:(i + 1) * nq, :] = (o / l).astype(o_ref.dtype)


def _na_attention(qk, v, tab, B, S, H):
    M = qk.shape[0]
    rows = S // GRID_W
    dh = NA_HEAD_DIM
    starts = tuple(_na_key_row_starts(rows))
    kern = functools.partial(_na_kernel, key_starts=starts, rows=rows)
    return pl.pallas_call(
        kern,
        grid=(H, B),
        in_specs=[pl.BlockSpec((S, dh), lambda h, b: (b, h)),
                  pl.BlockSpec((S, dh), lambda h, b: (b, H + h)),
                  pl.BlockSpec((S, dh), lambda h, b: (b, h)),
                  pl.BlockSpec((None,) + tab.shape[1:], lambda h, b: (h, 0, 0))],
        out_specs=pl.BlockSpec((S, dh), lambda h, b: (b, h)),
        out_shape=jax.ShapeDtypeStruct((M, H * dh), BF16),
        scratch_shapes=[pltpu.VMEM((len(starts), NA_QROWS * GRID_W, NA_KROWS * GRID_W), F32)],
        compiler_params=_params("arbitrary", "arbitrary"),
        name="na_attn",
    )(qk, qk, v, tab)


def _na_layer(x, norm_g, w_qkv, q_gain, k_gain, rpb, w_o, B, S):
    M, D = x.shape
    H = D // NA_HEAD_DIM
    rows = S // GRID_W
    assert rows % NA_QROWS == 0 and rows >= NA_KROWS
    wb = w_qkv.astype(BF16)
    head_gain = jnp.concatenate([jnp.tile(q_gain.astype(F32) * NA_HEAD_DIM ** -0.5, H),
                                 jnp.tile(k_gain.astype(F32), H)])
    qk = _proj(x, wb[:, :2 * D], n_out=2 * D, prologue="rms", p1=norm_g, head_gain=head_gain,
               out_dtype=BF16, name="na_qk")
    v = _proj(x, wb[:, 2 * D:], n_out=D, prologue="rms", p1=norm_g, out_dtype=BF16, name="na_v")
    o = _na_attention(qk, v, _na_column_bias(rpb), B, S, H)
    return _proj(o, w_o.astype(BF16), n_out=D, res=x, name="na_out")


def _s5_operators(a_re, a_im, log_dt, b_re, b_im, c_re, c_im):
    T = S5_CHUNK
    G, P = a_re.shape[1:]
    Cg = b_re.shape[-1]
    lam_re = jnp.minimum(a_re.astype(F32), -1e-4)
    lam_im = a_im.astype(F32)
    dt = jnp.exp(log_dt.astype(F32))[..., None]
    k = jnp.arange(T + 1, dtype=F32)[:, None, None, None]
    mag = jnp.exp(k * (lam_re * dt)[None])
    pw_r = mag * jnp.cos(k * (lam_im * dt)[None])
    pw_i = mag * jnp.sin(k * (lam_im * dt)[None])
    ab_r, ab_i = pw_r[1], pw_i[1]
    den = jnp.square(lam_re) + jnp.square(lam_im)
    w_r, w_i = ab_r - 1.0, ab_i
    f_r = (w_r * lam_re + w_i * lam_im) / den
    f_i = (w_i * lam_re - w_r * lam_im) / den
    br, bi = b_re.astype(F32), b_im.astype(F32)
    bb_r = f_r[..., None] * br - f_i[..., None] * bi
    bb_i = f_r[..., None] * bi + f_i[..., None] * br
    cr, ci = c_re.astype(F32), c_im.astype(F32)
    ca_r = cr[None] * pw_r[:, :, :, None, :] - ci[None] * pw_i[:, :, :, None, :]
    ca_i = cr[None] * pw_i[:, :, :, None, :] + ci[None] * pw_r[:, :, :, None, :]
    kern = (jnp.einsum("kdgcp,dgpe->dgeck", ca_r[:T], bb_r)
            - jnp.einsum("kdgcp,dgpe->dgeck", ca_i[:T], bb_i))
    kf, kb = kern[0], kern[1]
    lags = jnp.concatenate([jnp.flip(kb[..., 1:], axis=-1), kf[..., :1] + kb[..., :1], kf[..., 1:]], axis=-1)
    m = jnp.stack([lags[..., T - 1 - s:2 * T - 1 - s] for s in range(T)], axis=2)
    m = m.reshape(G, Cg * T, Cg * T)
    pf = np.arange(T)[::-1].copy()
    pb = np.arange(T)

    def state_in(d, order):
        pr, pi = pw_r[order, d], pw_i[order, d]
        e_r = pr[..., None] * bb_r[d][None] - pi[..., None] * bb_i[d][None]
        e_i = pr[..., None] * bb_i[d][None] + pi[..., None] * bb_r[d][None]
        tr = lambda e: jnp.transpose(e, (1, 3, 0, 2)).reshape(G, Cg * T, P)
        return tr(e_r), tr(e_i)

    ef_r, ef_i = state_in(0, pf)
    eb_r, eb_i = state_in(1, pb)
    e = jnp.concatenate([ef_r, eb_r, ef_i, eb_i], axis=-1)
    of = np.arange(1, T + 1)
    ob = np.arange(T, 0, -1)
    tr = lambda e: jnp.transpose(e, (1, 3, 2, 0)).reshape(G, P, Cg * T)
    w2r = jnp.concatenate([tr(ca_r[of, 0]), tr(ca_r[ob, 1])], axis=1)
    w2i = jnp.concatenate([-tr(ca_i[of, 0]), -tr(ca_i[ob, 1])], axis=1)
    at_r = jnp.concatenate([pw_r[T, 0], pw_r[T, 1]], axis=-1)[:, None, :]
    at_i = jnp.concatenate([pw_i[T, 0], pw_i[T, 1]], axis=-1)[:, None, :]
    return m.astype(BF16), e.astype(BF16), w2r.astype(BF16), w2i.astype(BF16), at_r, at_i


S5_GROUP_BLOCK = 8
S5_BATCH_BLOCK = 8


def _s5_kernel(u_ref, m_ref, e_ref, w2r_ref, w2i_ref, ar_ref, ai_ref, y_ref, p_ref, xr_ref, xi_ref):
    nchunk, nl, nbg, _ = u_ref.shape
    TC = nl * LANES
    ng = m_ref.shape[0]
    nb = nbg // ng
    R = nchunk * nb
    P2 = ar_ref.shape[-1]
    P = P2 // 2
    zeros = jnp.zeros((nb, P), F32)
    last = (nchunk - 1) * nb
    is_fwd = lax.broadcasted_iota(jnp.int32, (nb, P2), 1) < P
    for gi in range(ng):
        rows_g = pl.ds(gi, nb, stride=ng)
        lhs = jnp.concatenate([u_ref[:, pl.ds(l, 1), rows_g, :].reshape(R, LANES) for l in range(nl)], axis=-1)
        lhs = lhs.astype(BF16)
        p_ref[:, 0:TC] = jnp.dot(lhs, m_ref[gi], preferred_element_type=F32)
        p_ref[:, TC:] = jnp.dot(lhs, e_ref[gi], preferred_element_type=F32)
        xr_ref[0:nb, 0:P] = zeros
        xi_ref[0:nb, 0:P] = zeros
        xr_ref[last:last + nb, P:P2] = zeros
        xi_ref[last:last + nb, P:P2] = zeros
        ar = jnp.broadcast_to(ar_ref[gi], (nb, P2))
        ai = jnp.broadcast_to(ai_ref[gi], (nb, P2))

        def body(j, carry):
            xr, xi = carry
            rf = pl.multiple_of(j * nb, nb)
            rb = pl.multiple_of((nchunk - 1 - j) * nb, nb)
            in_r = jnp.where(is_fwd, p_ref[pl.ds(rf, nb), TC:TC + P2], p_ref[pl.ds(rb, nb), TC:TC + P2])
            in_i = jnp.where(is_fwd, p_ref[pl.ds(rf, nb), TC + P2:TC + 2 * P2],
                             p_ref[pl.ds(rb, nb), TC + P2:TC + 2 * P2])
            nxr = ar * xr - ai * xi + in_r
            nxi = ar * xi + ai * xr + in_i
            wf = pl.multiple_of((j + 1) * nb, nb)
            wb = pl.multiple_of((nchunk - 2 - j) * nb, nb)
            xr_ref[pl.ds(wf, nb), 0:P] = nxr[:, 0:P]
            xi_ref[pl.ds(wf, nb), 0:P] = nxi[:, 0:P]
            xr_ref[pl.ds(wb, nb), P:P2] = nxr[:, P:P2]
            xi_ref[pl.ds(wb, nb), P:P2] = nxi[:, P:P2]
            return nxr, nxi

        z = jnp.zeros((nb, P2), F32)
        lax.fori_loop(0, nchunk - 1, body, (z, z))
        y = p_ref[:, 0:TC]
        y = y + jnp.dot(xr_ref[...].astype(BF16), w2r_ref[gi], preferred_element_type=F32)
        y = y + jnp.dot(xi_ref[...].astype(BF16), w2i_ref[gi], preferred_element_type=F32)
        for l in range(nl):
            y_ref[:, pl.ds(l, 1), rows_g, :] = y[:, l * LANES:(l + 1) * LANES].reshape(nchunk, 1, nb, LANES)


def _s5_scan(ug, m, e, w2r, w2i, at_r, at_i):
    nchunk, nbb, ngb, nl, nbg, _ = ug.shape
    P2 = at_r.shape[-1]
    ng = m.shape[0] // ngb
    R = nchunk * (nbg // ng)
    blk = lambda a: pl.BlockSpec((ng,) + a.shape[1:], lambda g, b: (g, 0, 0))
    grp = pl.BlockSpec((nchunk, None, None, nl, nbg, LANES), lambda g, b: (0, b, g, 0, 0, 0))
    return pl.pallas_call(
        _s5_kernel,
        grid=(ngb, nbb),
        in_specs=[grp, blk(m), blk(e), blk(w2r), blk(w2i), blk(at_r), blk(at_i)],
        out_specs=grp,
        out_shape=jax.ShapeDtypeStruct(ug.shape, F32),
        scratch_shapes=[pltpu.VMEM((R, m.shape[-1] + e.shape[-1]), F32), pltpu.VMEM((R, P2), F32),
                        pltpu.VMEM((R, P2), F32)],
        compiler_params=_params("parallel", "arbitrary"),
        name="s5_scan",
    )(ug, m, e, w2r, w2i, at_r, at_i)


def _s5_group_tile(u_ref, o_ref, T):
    nc, ngb, nl, ng, _ = o_ref.shape
    TC = nl * LANES
    G = ngb * ng
    for ci in range(nc):
        rows = slice(ci * T, (ci + 1) * T)
        tile = jnp.concatenate([u_ref[rows, c * G:(c + 1) * G] for c in range(TC // T)], axis=0)
        t = tile.T.reshape(ngb, ng, TC)
        for l in range(nl):
            o_ref[ci, :, l] = t[:, :, l * LANES:(l + 1) * LANES]


def _s5_ungroup_tile(y_ref, u_ref, d_ref, o_ref, T):
    nc, ngb, nl, ng, _ = y_ref.shape
    TC = nl * LANES
    G = ngb * ng
    for ci in range(nc):
        rows = slice(ci * T, (ci + 1) * T)
        t = jnp.concatenate([y_ref[ci, :, l] for l in range(nl)], axis=-1)
        yt = t.reshape(G, TC).T
        for c in range(TC // T):
            cols = slice(c * G, (c + 1) * G)
            y = yt[c * T:(c + 1) * T, :] + d_ref[c:c + 1, :] * u_ref[rows, cols]
            o_ref[rows, cols] = jax.nn.gelu(y).astype(o_ref.dtype)


def _s5_grouped(B, nchunk, T, G, Cg, tm):
    TC = Cg * T
    nb = min(S5_BATCH_BLOCK, B)
    ng = min(S5_GROUP_BLOCK, G)
    nc = tm // T
    steps = nchunk // nc

    def index(i):
        b, c = i // steps, i % steps
        return (c, b // nb, 0, 0, b % nb, 0)

    shape = (nchunk, B // nb, G // ng, TC // LANES, nb * ng, LANES)
    return shape, pl.BlockSpec((nc, None, G // ng, TC // LANES, ng, LANES), index)


def _s5_layer(x, norm_g, w_in, a_re, a_im, log_dt, b_re, b_im, c_re, c_im, d_skip, w_glu, B, S, *, tm=512):
    M, D = x.shape
    T = S5_CHUNK
    G, Cg = d_skip.shape
    tm = min(tm, S)
    w_in_p = w_in.reshape(D, G, Cg).transpose(0, 2, 1).reshape(D, D).astype(BF16)
    w_glu_p = w_glu.reshape(G, Cg, -1).transpose(1, 0, 2).reshape(D, -1).astype(BF16)
    grouped = _s5_grouped(B, S // T, T, G, Cg, tm)
    u, ug = _proj(x, w_in_p, n_out=D, prologue="rms", p1=norm_g, tm=tm, name="s5_in",
                  s5_chunk=T, s5_group_out=grouped)
    ops = _s5_operators(a_re, a_im, log_dt, b_re, b_im, c_re, c_im)
    yg = _s5_scan(ug, *ops)
    return _proj(u, w_glu_p, n_out=D, prologue="s5_act", p1=d_skip.T, glu=True, res=x, tm=tm,
                 name="s5_out", s5_chunk=T, s5_grouped=(yg, grouped[1]))


def kernel(x, mix_norm, fnet_w, fnet_b, conv_w_in, conv_b_in, conv_w_dw, conv_b_dw, conv_ln_g,
           conv_ln_b, conv_w_out, conv_b_out, na_w_qkv, na_q_gain, na_k_gain, na_rpb, na_w_o,
           s5_w_in, s5_a_re, s5_a_im, s5_log_dt, s5_b_re, s5_b_im, s5_c_re, s5_c_im, s5_d,
           s5_w_glu, mlp_norm, mlp_w1, mlp_w2):
    B, S, D = x.shape
    depth = mix_norm.shape[0]
    h = x.reshape(B * S, D)
    w1_bf, w2_bf = mlp_w1.astype(BF16), mlp_w2.astype(BF16)
    for i in range(depth):
        kind, j = i % 4, i // 4
        if kind == 0:
            h = _fourier_layer(h, mix_norm[i], fnet_w[j], fnet_b[j], B, S)
        elif kind == 1:
            h = _conv_layer(h, mix_norm[i], conv_w_in[j], conv_b_in[j], conv_w_dw[j], conv_b_dw[j],
                            conv_ln_g[j], conv_ln_b[j], conv_w_out[j], conv_b_out[j], B, S)
        elif kind == 2:
            h = _na_layer(h, mix_norm[i], na_w_qkv[j], na_q_gain[j], na_k_gain[j], na_rpb[j],
                          na_w_o[j], B, S)
        else:
            h = _s5_layer(h, mix_norm[i], s5_w_in[j], s5_a_re[j], s5_a_im[j], s5_log_dt[j],
                          s5_b_re[j], s5_b_im[j], s5_c_re[j], s5_c_im[j], s5_d[j], s5_w_glu[j], B, S)
        h = _mlp(h, mlp_norm[i], w1_bf, w2_bf, i)
    return h.reshape(B, S, D)
```

```python
import functools
import math

import numpy as np
import jax
import jax.numpy as jnp
from jax import lax
from jax.experimental import pallas as pl
from jax.experimental.pallas import tpu as pltpu

NORM_EPS = 1e-6
FNET_GROUPS = 8
CONV_WIDTH = 31
GRID_W = 64
NA_HEAD_DIM = 128
NA_WIN_H = 8
NA_WIN_W = 16
NA_QROWS = 8
NA_KROWS = 16
S5_CHUNK = 16
MASK_VALUE = -1e30

V7X_VMEM_LIMIT_BYTES = 56 * 1024 * 1024
SUBLANES = 8
LANES = 128
BF16 = jnp.bfloat16
F32 = jnp.float32


def _params(*sem):
    return pltpu.CompilerParams(dimension_semantics=sem, vmem_limit_bytes=V7X_VMEM_LIMIT_BYTES)


def _rms(x, g):
    return x * lax.rsqrt(jnp.mean(x * x, axis=-1, keepdims=True) + NORM_EPS) * g


def _proj_kernel(*refs, prologue, glu, has_bias, headnorm, has_res, tn, s5_chunk):
    it = iter(refs)
    x_ref = next(it)
    yg_ref = next(it) if prologue == "s5_act" else None
    p1_ref = next(it) if prologue != "none" else None
    p2_ref = next(it) if prologue == "ln_silu" else None
    w_ref = next(it)
    b_ref = next(it) if has_bias else None
    hg_ref = next(it) if headnorm else None
    r_ref = next(it) if has_res else None
    o_ref = next(it)
    ug_ref = next(it) if (s5_chunk and prologue != "s5_act") else None
    xn_ref = next(it) if prologue != "none" else None

    if prologue == "s5_act":
        _s5_ungroup_tile(yg_ref, x_ref, p1_ref, xn_ref, s5_chunk)
        lhs_ref = xn_ref
    elif prologue != "none":
        x = x_ref[...].astype(F32)
        if prologue == "rms":
            y = _rms(x, p1_ref[...])
        else:
            mu = jnp.mean(x, axis=-1, keepdims=True)
            xc = x - mu
            y = xc * lax.rsqrt(jnp.mean(xc * xc, axis=-1, keepdims=True) + NORM_EPS)
            y = y * p1_ref[...] + p2_ref[...]
            y = y * jax.nn.sigmoid(y)
        xn_ref[...] = y.astype(BF16)
        lhs_ref = xn_ref
    else:
        lhs_ref = x_ref

    n_out = o_ref.shape[1]
    for j in range(n_out // tn):
        sl = slice(j * tn, (j + 1) * tn)
        xn = lhs_ref[...].astype(BF16)
        acc = jnp.dot(xn, w_ref[:, sl], preferred_element_type=F32)
        if has_bias:
            acc = acc + b_ref[:, sl]
        if glu:
            slg = slice(n_out + j * tn, n_out + (j + 1) * tn)
            gate = jnp.dot(xn, w_ref[:, slg], preferred_element_type=F32)
            if has_bias:
                gate = gate + b_ref[:, slg]
            acc = acc * jax.nn.sigmoid(gate)
        if has_res:
            acc = acc + r_ref[:, sl]
        if headnorm and j * tn < hg_ref.shape[1]:
            for h in range(tn // NA_HEAD_DIM):
                hs = slice(j * tn + h * NA_HEAD_DIM, j * tn + (h + 1) * NA_HEAD_DIM)
                hl = slice(h * NA_HEAD_DIM, (h + 1) * NA_HEAD_DIM)
                o_ref[:, hs] = _rms(acc[:, hl], hg_ref[:, hs]).astype(o_ref.dtype)
        else:
            o_ref[:, sl] = acc.astype(o_ref.dtype)
    if ug_ref is not None:
        _s5_group_tile(o_ref, ug_ref, s5_chunk)


def _proj(x, w, *, n_out, prologue="none", p1=None, p2=None, glu=False, bias=None,
          head_gain=None, res=None, out_dtype=F32, tm=512, tn=512, name="proj",
          s5_chunk=0, s5_grouped=None, s5_group_out=None):
    M, K = x.shape
    tm = min(tm, M)
    tn = min(tn, n_out)
    row = pl.BlockSpec((tm, K), lambda i: (i, 0))
    orow = pl.BlockSpec((tm, n_out), lambda i: (i, 0))
    const = lambda a: pl.BlockSpec(a.shape, lambda i: (0,) * a.ndim, pipeline_mode=pl.Buffered(1))
    args, specs = [x], [row]

    def add_const(a):
        args.append(a); specs.append(const(a))

    if prologue == "s5_act":
        args.append(s5_grouped[0]); specs.append(s5_grouped[1])
        add_const(p1.astype(F32))
    elif prologue != "none":
        add_const(p1.reshape(1, K).astype(F32))
    if prologue == "ln_silu":
        add_const(p2.reshape(1, K).astype(F32))
    add_const(w)
    if bias is not None:
        add_const(bias.reshape(1, -1).astype(F32))
    if head_gain is not None:
        add_const(head_gain.reshape(1, -1).astype(F32))
    if res is not None:
        args.append(res); specs.append(orow)
    scratch = [pltpu.VMEM((tm, K), BF16)] if prologue != "none" else []
    kern = functools.partial(_proj_kernel, prologue=prologue, glu=glu, has_bias=bias is not None,
                             headnorm=head_gain is not None, has_res=res is not None, tn=tn,
                             s5_chunk=s5_chunk)
    out_specs, out_shape = orow, jax.ShapeDtypeStruct((M, n_out), out_dtype)
    if s5_group_out is not None:
        out_specs = [orow, s5_group_out[1]]
        out_shape = [out_shape, jax.ShapeDtypeStruct(s5_group_out[0], F32)]
    return pl.pallas_call(
        kern,
        grid=(M // tm,),
        in_specs=specs,
        out_specs=out_specs,
        out_shape=out_shape,
        scratch_shapes=scratch,
        compiler_params=_params("parallel"),
        name=name,
    )(*args)


def _mlp_kernel(x_ref, g_ref, w1_ref, w2_ref, o_ref, xn_ref):
    @pl.when(pl.program_id(1) == 0)
    def _():
        x = x_ref[...]
        xn_ref[...] = _rms(x, g_ref[...]).astype(BF16)
        o_ref[...] = x

    h = jnp.dot(xn_ref[...], w1_ref[...], preferred_element_type=F32)
    h = jnp.square(jnp.maximum(h, 0.0)).astype(BF16)
    o_ref[...] += jnp.dot(h, w2_ref[...], preferred_element_type=F32)


def _mlp(x, g, w1, w2, layer, *, tm=1024, tf=512):
    M, D = x.shape
    F = w1.shape[2]
    tm = min(tm, M)
    tf = min(tf, F)
    return pl.pallas_call(
        _mlp_kernel,
        grid=(M // tm, F // tf),
        in_specs=[pl.BlockSpec((tm, D), lambda i, f: (i, 0)),
                  pl.BlockSpec((1, D), lambda i, f: (0, 0)),
                  pl.BlockSpec((None, D, tf), lambda i, f: (layer, 0, f)),
                  pl.BlockSpec((None, tf, D), lambda i, f: (layer, f, 0))],
        out_specs=pl.BlockSpec((tm, D), lambda i, f: (i, 0)),
        out_shape=jax.ShapeDtypeStruct((M, D), F32),
        scratch_shapes=[pltpu.VMEM((tm, D), BF16)],
        compiler_params=_params("parallel", "arbitrary"),
        name="mlp",
    )(x, g.reshape(1, D).astype(F32), w1, w2)


def _dft_tables(n, scale, rows=None):
    k = jnp.arange(n if rows is None else rows, dtype=jnp.int32)
    s = jnp.arange(n, dtype=jnp.int32)
    ang = ((k[:, None] * s[None, :]) % n).astype(F32) * (2.0 * math.pi / n)
    return jnp.cos(ang) * scale, jnp.sin(ang) * scale


def _fnet_chan_kernel(x_ref, g_ref, t_ref, a_ref, b_ref):
    gc = t_ref.shape[0]
    xn = _rms(x_ref[...], g_ref[...]).astype(BF16)
    for g in range(xn.shape[1] // gc):
        sl = slice(g * gc, (g + 1) * gc)
        r = jnp.dot(xn[:, sl], t_ref[...], preferred_element_type=F32)
        a_ref[:, sl] = r[:, :gc].astype(BF16)
        b_ref[:, sl] = r[:, gc:].astype(BF16)


def _fnet_chan(x, g, table, *, tm=512):
    M, D = x.shape
    gc = table.shape[0]
    tm = min(tm, M)
    row = pl.BlockSpec((tm, D), lambda i: (i, 0))
    return pl.pallas_call(
        _fnet_chan_kernel,
        grid=(M // tm,),
        in_specs=[row, pl.BlockSpec((1, D), lambda i: (0, 0)),
                  pl.BlockSpec((gc, 2 * gc), lambda i: (0, 0))],
        out_specs=[row, row],
        out_shape=[jax.ShapeDtypeStruct((M, D), BF16)] * 2,
        compiler_params=_params("parallel"),
        name="fnet_chan",
    )(x, g.reshape(1, D).astype(F32), table)


def _fnet_seq_kernel(cs_ref, ss_ref, cmid_ref, rev_ref, a_ref, b_ref, z_ref):
    half = cs_ref.shape[0]
    a = a_ref[...]
    pm = jnp.dot(cs_ref[...], a, preferred_element_type=F32)
    qm = jnp.dot(ss_ref[...], b_ref[...], preferred_element_type=F32)
    z_ref[0:half, :] = (pm - qm).astype(z_ref.dtype)
    upper = jnp.dot(rev_ref[...], (pm + qm).astype(BF16), preferred_element_type=F32)
    mid = jnp.dot(cmid_ref[...], a, preferred_element_type=F32)[0:1, :]
    row = lax.broadcasted_iota(jnp.int32, (half, 1), 0)
    z_ref[half:2 * half, :] = jnp.where(row == 0, mid, upper).astype(z_ref.dtype)


def _fnet_seq(cs, ss, a, b, *, tn=512):
    B, S, D = a.shape
    half = S // 2
    tn = min(tn, D)
    k = jnp.arange(half, dtype=jnp.int32)
    rev = (k[:, None] + k[None, :] == half).astype(BF16)
    const = lambda arr: pl.BlockSpec(arr.shape, lambda bi, j: (0, 0))
    colblk = pl.BlockSpec((None, S, tn), lambda bi, j: (bi, 0, j))
    tabs = [cs[:half], ss[:half], cs[half:half + SUBLANES], rev]
    return pl.pallas_call(
        _fnet_seq_kernel,
        grid=(B, D // tn),
        in_specs=[const(t) for t in tabs] + [colblk, colblk],
        out_specs=colblk,
        out_shape=jax.ShapeDtypeStruct((B, S, D), BF16),
        compiler_params=_params("parallel", "arbitrary"),
        name="fnet_seq",
    )(*tabs, a, b)


def _fourier_layer(x, norm_g, w, b, B, S):
    M, D = x.shape
    gc = D // FNET_GROUPS
    cc, sc = _dft_tables(gc, gc ** -0.5)
    cs, ss = _dft_tables(S, S ** -0.5, rows=S // 2 + SUBLANES)
    table = jnp.concatenate([cc, sc], axis=1).astype(BF16)
    a, bm = _fnet_chan(x, norm_g, table)
    z = _fnet_seq(cs.astype(BF16), ss.astype(BF16), a.reshape(B, S, D), bm.reshape(B, S, D))
    return _proj(z.reshape(M, D), w.astype(BF16), n_out=D, bias=b, res=x, name="fnet_out")


CONV_HALO = 16
CONV_ROWS = 64


def _dwconv_kernel(u_ref, w_ref, b_ref, o_ref, pad_ref):
    S, tc = u_ref.shape
    zeros = jnp.zeros((CONV_HALO, tc), F32)
    pad_ref[0:CONV_HALO, :] = zeros
    pad_ref[CONV_HALO + S:2 * CONV_HALO + S, :] = zeros
    pad_ref[CONV_HALO:CONV_HALO + S, :] = u_ref[...]
    first = CONV_HALO - CONV_WIDTH // 2

    def body(r, carry):
        base = pl.multiple_of(r * CONV_ROWS, CONV_ROWS)
        acc = jnp.broadcast_to(b_ref[...], (CONV_ROWS, tc))
        for res in range(SUBLANES):
            nrows = CONV_ROWS + (SUBLANES if res else 0)
            part = None
            for k in range(CONV_WIDTH):
                off = first + k
                if off % SUBLANES == res:
                    start = pl.multiple_of(base + (off - res), SUBLANES)
                    term = pad_ref[pl.ds(start, nrows), :] * w_ref[k:k + 1, :]
                    part = term if part is None else part + term
            if part is not None:
                acc = acc + part[res:res + CONV_ROWS, :]
        o_ref[pl.ds(base, CONV_ROWS), :] = acc
        return carry

    lax.fori_loop(0, S // CONV_ROWS, body, 0)


def _dwconv(u, w_dw, b_dw, *, tc=256):
    B, S, D = u.shape
    tc = min(tc, D)
    blk = pl.BlockSpec((None, S, tc), lambda bi, c: (bi, 0, c))
    return pl.pallas_call(
        _dwconv_kernel,
        grid=(B, D // tc),
        in_specs=[blk, pl.BlockSpec((CONV_WIDTH, tc), lambda bi, c: (0, c)),
                  pl.BlockSpec((1, tc), lambda bi, c: (0, c))],
        out_specs=blk,
        out_shape=jax.ShapeDtypeStruct((B, S, D), F32),
        scratch_shapes=[pltpu.VMEM((S + 2 * CONV_HALO, tc), F32)],
        compiler_params=_params("parallel", "parallel"),
        name="dwconv",
    )(u, w_dw.astype(F32), b_dw.reshape(1, D).astype(F32))


def _conv_layer(x, norm_g, w_in, b_in, w_dw, b_dw, ln_g, ln_b, w_out, b_out, B, S):
    M, D = x.shape
    u = _proj(x, w_in.astype(BF16), n_out=D, prologue="rms", p1=norm_g, glu=True, bias=b_in,
              name="conv_in")
    v = _dwconv(u.reshape(B, S, D), w_dw, b_dw).reshape(M, D)
    return _proj(v, w_out.astype(BF16), n_out=D, prologue="ln_silu", p1=ln_g, p2=ln_b,
                 bias=b_out, res=x, name="conv_out")


def _na_key_row_starts(rows):
    nblk = rows // NA_QROWS
    return [int(np.clip(i * NA_QROWS - NA_WIN_H // 2, 0, rows - NA_KROWS)) for i in range(nblk)]


NA_TAB = 18 * GRID_W


def _na_column_bias(rpb):
    H, ndr, _ = rpb.shape
    qc = np.arange(GRID_W)
    kc = np.arange(GRID_W)
    cstart = np.clip(qc - NA_WIN_W // 2, 0, GRID_W - NA_WIN_W)
    vcol = (kc[None, :] >= cstart[:, None]) & (kc[None, :] < cstart[:, None] + NA_WIN_W)
    dc = np.clip(kc[None, :] - qc[:, None] + NA_WIN_W - 1, 0, 2 * NA_WIN_W - 2)
    t = jnp.where(jnp.asarray(vcol)[None, None], rpb.astype(F32)[:, :, dc], MASK_VALUE)
    t = jnp.transpose(t, (0, 2, 1, 3)).reshape(H, GRID_W, ndr * GRID_W)
    pad = lambda n: jnp.full((H, GRID_W, n * GRID_W), MASK_VALUE, F32)
    nblk = NA_TAB // GRID_W
    tab_a = jnp.concatenate([t, pad(nblk - ndr)], axis=-1)
    tab_b = jnp.concatenate([pad(1), t, pad(nblk - ndr - 1)], axis=-1)
    return jnp.concatenate([tab_a, tab_b], axis=-1)


def _na_build_bias(tab_ref, bias_ref, key_starts, rows):
    kh = min(NA_WIN_H, rows)
    pair = 2 * GRID_W
    lane = lax.broadcasted_iota(jnp.int32, (GRID_W, pair), 1)
    masked = jnp.full((GRID_W, pair), MASK_VALUE, F32)
    for i, ks in enumerate(key_starts):
        for qr in range(NA_QROWS):
            r = i * NA_QROWS + qr
            r0 = int(np.clip(r - kh // 2, 0, rows - kh))
            for kp in range(NA_KROWS // 2):
                kabs = ks + 2 * kp
                v0 = r0 <= kabs < r0 + kh
                v1 = r0 <= kabs + 1 < r0 + kh
                dr = kabs - r + NA_WIN_H - 1
                if not (v0 or v1):
                    tile = masked
                else:
                    start = dr * GRID_W if dr % 2 == 0 else NA_TAB + (dr + 1) * GRID_W
                    tile = tab_ref[:, start:start + pair]
                    if not v1:
                        tile = jnp.where(lane < GRID_W, tile, MASK_VALUE)
                    elif not v0:
                        tile = jnp.where(lane >= GRID_W, tile, MASK_VALUE)
                bias_ref[i, qr * GRID_W:(qr + 1) * GRID_W, kp * pair:(kp + 1) * pair] = tile


def _na_kernel(q_ref, k_ref, v_ref, tab_ref, o_ref, bias_ref, *, key_starts, rows):
    @pl.when(pl.program_id(1) == 0)
    def _():
        _na_build_bias(tab_ref, bias_ref, key_starts, rows)

    nq = NA_QROWS * GRID_W
    nk = NA_KROWS * GRID_W
    for i, ks in enumerate(key_starts):
        q = q_ref[i * nq:(i + 1) * nq, :]
        k = k_ref[ks * GRID_W:ks * GRID_W + nk, :]
        v = v_ref[ks * GRID_W:ks * GRID_W + nk, :]
        s = lax.dot_general(q, k, (((1,), (1,)), ((), ())), preferred_element_type=F32)
        s = s + bias_ref[i]
        p = jnp.exp(s - jnp.max(s, axis=-1, keepdims=True))
        l = jnp.sum(p, axis=-1, keepdims=True)
        o = jnp.dot(p.astype(BF16), v, preferred_element_type=F32)
        o_ref[i * nq:(i + 1) * nq, :] = (o / l).astype(o_ref.dtype)


def _na_attention(qkv, tab, B, S, H):
    M = qkv.shape[0]
    rows = S // GRID_W
    dh = NA_HEAD_DIM
    starts = tuple(_na_key_row_starts(rows))
    kern = functools.partial(_na_kernel, key_starts=starts, rows=rows)
    return pl.pallas_call(
        kern,
        grid=(H, B),
        in_specs=[pl.BlockSpec((S, dh), lambda h, b: (b, h)),
                  pl.BlockSpec((S, dh), lambda h, b: (b, H + h)),
                  pl.BlockSpec((S, dh), lambda h, b: (b, 2 * H + h)),
                  pl.BlockSpec((None,) + tab.shape[1:], lambda h, b: (h, 0, 0))],
        out_specs=pl.BlockSpec((S, dh), lambda h, b: (b, h)),
        out_shape=jax.ShapeDtypeStruct((M, H * dh), BF16),
        scratch_shapes=[pltpu.VMEM((len(starts), NA_QROWS * GRID_W, NA_KROWS * GRID_W), F32)],
        compiler_params=_params("arbitrary", "arbitrary"),
        name="na_attn",
    )(qkv, qkv, qkv, tab)


def _na_layer(x, norm_g, w_qkv, q_gain, k_gain, rpb, w_o, B, S):
    M, D = x.shape
    H = D // NA_HEAD_DIM
    rows = S // GRID_W
    assert rows % NA_QROWS == 0 and rows >= NA_KROWS
    wb = w_qkv.astype(BF16)
    head_gain = jnp.concatenate([jnp.tile(q_gain.astype(F32) * NA_HEAD_DIM ** -0.5, H),
                                 jnp.tile(k_gain.astype(F32), H)])
    qkv = _proj(x, wb, n_out=3 * D, prologue="rms", p1=norm_g, head_gain=head_gain,
                out_dtype=BF16, name="na_qkv")
    o = _na_attention(qkv, _na_column_bias(rpb), B, S, H)
    return _proj(o, w_o.astype(BF16), n_out=D, res=x, name="na_out")


def _s5_operators(a_re, a_im, log_dt, b_re, b_im, c_re, c_im):
    T = S5_CHUNK
    G, P = a_re.shape[1:]
    Cg = b_re.shape[-1]
    lam_re = jnp.minimum(a_re.astype(F32), -1e-4)
    lam_im = a_im.astype(F32)
    dt = jnp.exp(log_dt.astype(F32))[..., None]
    k = jnp.arange(T + 1, dtype=F32)[:, None, None, None]
    mag = jnp.exp(k * (lam_re * dt)[None])
    pw_r = mag * jnp.cos(k * (lam_im * dt)[None])
    pw_i = mag * jnp.sin(k * (lam_im * dt)[None])
    ab_r, ab_i = pw_r[1], pw_i[1]
    den = jnp.square(lam_re) + jnp.square(lam_im)
    w_r, w_i = ab_r - 1.0, ab_i
    f_r = (w_r * lam_re + w_i * lam_im) / den
    f_i = (w_i * lam_re - w_r * lam_im) / den
    br, bi = b_re.astype(F32), b_im.astype(F32)
    bb_r = f_r[..., None] * br - f_i[..., None] * bi
    bb_i = f_r[..., None] * bi + f_i[..., None] * br
    cr, ci = c_re.astype(F32), c_im.astype(F32)
    ca_r = cr[None] * pw_r[:, :, :, None, :] - ci[None] * pw_i[:, :, :, None, :]
    ca_i = cr[None] * pw_i[:, :, :, None, :] + ci[None] * pw_r[:, :, :, None, :]
    kern = (jnp.einsum("kdgcp,dgpe->dgeck", ca_r[:T], bb_r)
            - jnp.einsum("kdgcp,dgpe->dgeck", ca_i[:T], bb_i))
    kf, kb = kern[0], kern[1]
    lags = jnp.concatenate([jnp.flip(kb[..., 1:], axis=-1), kf[..., :1] + kb[..., :1], kf[..., 1:]], axis=-1)
    m = jnp.stack([lags[..., T - 1 - s:2 * T - 1 - s] for s in range(T)], axis=2)
    m = m.reshape(G, Cg * T, Cg * T)
    def state_in(d, reverse):
        pr, pi = pw_r[:T, d], pw_i[:T, d]
        if reverse:
            pr, pi = jnp.flip(pr, axis=0), jnp.flip(pi, axis=0)
        e_r = pr[..., None] * bb_r[d][None] - pi[..., None] * bb_i[d][None]
        e_i = pr[..., None] * bb_i[d][None] + pi[..., None] * bb_r[d][None]
        tr = lambda e: jnp.transpose(e, (1, 3, 0, 2)).reshape(G, Cg * T, P)
        return tr(e_r), tr(e_i)

    ef_r, ef_i = state_in(0, True)
    eb_r, eb_i = state_in(1, False)
    e = jnp.concatenate([ef_r, eb_r, ef_i, eb_i], axis=-1)
    tr = lambda e: jnp.transpose(e, (1, 3, 2, 0)).reshape(G, P, Cg * T)
    fwd = lambda ca: tr(ca[1:T + 1, 0])
    bwd = lambda ca: tr(jnp.flip(ca[1:T + 1, 1], axis=0))
    w2r = jnp.concatenate([fwd(ca_r), bwd(ca_r)], axis=1)
    w2i = jnp.concatenate([-fwd(ca_i), -bwd(ca_i)], axis=1)
    at_r = jnp.concatenate([pw_r[T, 0], pw_r[T, 1]], axis=-1)[:, None, :]
    at_i = jnp.concatenate([pw_i[T, 0], pw_i[T, 1]], axis=-1)[:, None, :]
    return m.astype(BF16), e.astype(BF16), w2r.astype(BF16), w2i.astype(BF16), at_r, at_i


S5_GROUP_BLOCK = 8
S5_BATCH_BLOCK = 8


def _s5_kernel(u_ref, m_ref, e_ref, w2r_ref, w2i_ref, ar_ref, ai_ref, y_ref, p_ref, xr_ref, xi_ref):
    nchunk, nl, nbg, _ = u_ref.shape
    TC = nl * LANES
    ng = m_ref.shape[0]
    nb = nbg // ng
    R = nchunk * nb
    P2 = ar_ref.shape[-1]
    P = P2 // 2
    zeros = jnp.zeros((nb, P), F32)
    last = (nchunk - 1) * nb
    is_fwd = lax.broadcasted_iota(jnp.int32, (nb, P2), 1) < P
    for gi in range(ng):
        rows_g = pl.ds(gi, nb, stride=ng)
        lhs = jnp.concatenate([u_ref[:, pl.ds(l, 1), rows_g, :].reshape(R, LANES) for l in range(nl)], axis=-1)
        lhs = lhs.astype(BF16)
        p_ref[:, 0:TC] = jnp.dot(lhs, m_ref[gi], preferred_element_type=F32)
        p_ref[:, TC:] = jnp.dot(lhs, e_ref[gi], preferred_element_type=F32)
        xr_ref[0:nb, 0:P] = zeros
        xi_ref[0:nb, 0:P] = zeros
        xr_ref[last:last + nb, P:P2] = zeros
        xi_ref[last:last + nb, P:P2] = zeros
        ar = jnp.broadcast_to(ar_ref[gi], (nb, P2))
        ai = jnp.broadcast_to(ai_ref[gi], (nb, P2))

        def body(j, carry):
            xr, xi = carry
            rf = pl.multiple_of(j * nb, nb)
            rb = pl.multiple_of((nchunk - 1 - j) * nb, nb)
            in_r = jnp.where(is_fwd, p_ref[pl.ds(rf, nb), TC:TC + P2], p_ref[pl.ds(rb, nb), TC:TC + P2])
            in_i = jnp.where(is_fwd, p_ref[pl.ds(rf, nb), TC + P2:TC + 2 * P2],
                             p_ref[pl.ds(rb, nb), TC + P2:TC + 2 * P2])
            nxr = ar * xr - ai * xi + in_r
            nxi = ar * xi + ai * xr + in_i
            wf = pl.multiple_of((j + 1) * nb, nb)
            wb = pl.multiple_of((nchunk - 2 - j) * nb, nb)
            xr_ref[pl.ds(wf, nb), 0:P] = nxr[:, 0:P]
            xi_ref[pl.ds(wf, nb), 0:P] = nxi[:, 0:P]
            xr_ref[pl.ds(wb, nb), P:P2] = nxr[:, P:P2]
            xi_ref[pl.ds(wb, nb), P:P2] = nxi[:, P:P2]
            return nxr, nxi

        z = jnp.zeros((nb, P2), F32)
        lax.fori_loop(0, nchunk - 1, body, (z, z))
        y = p_ref[:, 0:TC]
        y = y + jnp.dot(xr_ref[...].astype(BF16), w2r_ref[gi], preferred_element_type=F32)
        y = y + jnp.dot(xi_ref[...].astype(BF16), w2i_ref[gi], preferred_element_type=F32)
        for l in range(nl):
            y_ref[:, pl.ds(l, 1), rows_g, :] = y[:, l * LANES:(l + 1) * LANES].reshape(nchunk, 1, nb, LANES)


def _s5_scan(ug, m, e, w2r, w2i, at_r, at_i):
    nchunk, nbb, ngb, nl, nbg, _ = ug.shape
    P2 = at_r.shape[-1]
    ng = m.shape[0] // ngb
    R = nchunk * (nbg // ng)
    blk = lambda a: pl.BlockSpec((ng,) + a.shape[1:], lambda g, b: (g, 0, 0))
    grp = pl.BlockSpec((nchunk, None, None, nl, nbg, LANES), lambda g, b: (0, b, g, 0, 0, 0))
    return pl.pallas_call(
        _s5_kernel,
        grid=(ngb, nbb),
        in_specs=[grp, blk(m), blk(e), blk(w2r), blk(w2i), blk(at_r), blk(at_i)],
        out_specs=grp,
        out_shape=jax.ShapeDtypeStruct(ug.shape, F32),
        scratch_shapes=[pltpu.VMEM((R, m.shape[-1] + e.shape[-1]), F32), pltpu.VMEM((R, P2), F32),
                        pltpu.VMEM((R, P2), F32)],
        compiler_params=_params("parallel", "arbitrary"),
        name="s5_scan",
    )(ug, m, e, w2r, w2i, at_r, at_i)


def _s5_group_tile(u_ref, o_ref, T):
    nc, ngb, nl, ng, _ = o_ref.shape
    TC = nl * LANES
    G = ngb * ng
    for ci in range(nc):
        rows = slice(ci * T, (ci + 1) * T)
        tile = jnp.concatenate([u_ref[rows, c * G:(c + 1) * G] for c in range(TC // T)], axis=0)
        t = tile.T.reshape(ngb, ng, TC)
        for l in range(nl):
            o_ref[ci, :, l] = t[:, :, l * LANES:(l + 1) * LANES]


def _s5_ungroup_tile(y_ref, u_ref, d_ref, o_ref, T):
    nc, ngb, nl, ng, _ = y_ref.shape
    TC = nl * LANES
    G = ngb * ng
    for ci in range(nc):
        rows = slice(ci * T, (ci + 1) * T)
        t = jnp.concatenate([y_ref[ci, :, l] for l in range(nl)], axis=-1)
        yt = t.reshape(G, TC).T
        for c in range(TC // T):
            cols = slice(c * G, (c + 1) * G)
            y = yt[c * T:(c + 1) * T, :] + d_ref[c:c + 1, :] * u_ref[rows, cols]
            o_ref[rows, cols] = jax.nn.gelu(y).astype(o_ref.dtype)


def _s5_grouped(B, nchunk, T, G, Cg, tm):
    TC = Cg * T
    nb = min(S5_BATCH_BLOCK, B)
    ng = min(S5_GROUP_BLOCK, G)
    nc = tm // T
    steps = nchunk // nc

    def index(i):
        b, c = i // steps, i % steps
        return (c, b // nb, 0, 0, b % nb, 0)

    shape = (nchunk, B // nb, G // ng, TC // LANES, nb * ng, LANES)
    return shape, pl.BlockSpec((nc, None, G // ng, TC // LANES, ng, LANES), index)


def _s5_layer(x, norm_g, w_in, a_re, a_im, log_dt, b_re, b_im, c_re, c_im, d_skip, w_glu, B, S, *, tm=512):
    M, D = x.shape
    T = S5_CHUNK
    G, Cg = d_skip.shape
    tm = min(tm, S)
    w_in_p = w_in.reshape(D, G, Cg).transpose(0, 2, 1).reshape(D, D).astype(BF16)
    w_glu_p = w_glu.reshape(G, Cg, -1).transpose(1, 0, 2).reshape(D, -1).astype(BF16)
    grouped = _s5_grouped(B, S // T, T, G, Cg, tm)
    u, ug = _proj(x, w_in_p, n_out=D, prologue="rms", p1=norm_g, tm=tm, name="s5_in",
                  s5_chunk=T, s5_group_out=grouped)
    ops = _s5_operators(a_re, a_im, log_dt, b_re, b_im, c_re, c_im)
    yg = _s5_scan(ug, *ops)
    return _proj(u, w_glu_p, n_out=D, prologue="s5_act", p1=d_skip.T, glu=True, res=x, tm=tm,
                 name="s5_out", s5_chunk=T, s5_grouped=(yg, grouped[1]))


def kernel(x, mix_norm, fnet_w, fnet_b, conv_w_in, conv_b_in, conv_w_dw, conv_b_dw, conv_ln_g,
           conv_ln_b, conv_w_out, conv_b_out, na_w_qkv, na_q_gain, na_k_gain, na_rpb, na_w_o,
           s5_w_in, s5_a_re, s5_a_im, s5_log_dt, s5_b_re, s5_b_im, s5_c_re, s5_c_im, s5_d,
           s5_w_glu, mlp_norm, mlp_w1, mlp_w2):
    B, S, D = x.shape
    depth = mix_norm.shape[0]
    h = x.reshape(B * S, D)
    w1_bf, w2_bf = mlp_w1.astype(BF16), mlp_w2.astype(BF16)
    for i in range(depth):
        kind, j = i % 4, i // 4
        if kind == 0:
            h = _fourier_layer(h, mix_norm[i], fnet_w[j], fnet_b[j], B, S)
        elif kind == 1:
            h = _conv_layer(h, mix_norm[i], conv_w_in[j], conv_b_in[j], conv_w_dw[j], conv_b_dw[j],
                            conv_ln_g[j], conv_ln_b[j], conv_w_out[j], conv_b_out[j], B, S)
        elif kind == 2:
            h = _na_layer(h, mix_norm[i], na_w_qkv[j], na_q_gain[j], na_k_gain[j], na_rpb[j],
                          na_w_o[j], B, S)
        else:
            h = _s5_layer(h, mix_norm[i], s5_w_in[j], s5_a_re[j], s5_a_im[j], s5_log_dt[j],
                          s5_b_re[j], s5_b_im[j], s5_c_re[j], s5_c_im[j], s5_d[j], s5_w_glu[j], B, S)
        h = _mlp(h, mlp_norm[i], w1_bf, w2_bf, i)
    return h.reshape(B, S, D)
```

```python
import functools
import math

import numpy as np
import jax
import jax.numpy as jnp
from jax import lax
from jax.experimental import pallas as pl
from jax.experimental.pallas import tpu as pltpu

NORM_EPS = 1e-6
FNET_GROUPS = 8
CONV_WIDTH = 31
GRID_W = 64
NA_HEAD_DIM = 128
NA_WIN_H = 8
NA_WIN_W = 16
NA_QROWS = 8
NA_KROWS = 16
S5_CHUNK = 16
MASK_VALUE = -1e30

V7X_VMEM_LIMIT_BYTES = 56 * 1024 * 1024
SUBLANES = 8
LANES = 128
BF16 = jnp.bfloat16
F32 = jnp.float32


def _params(*sem):
    return pltpu.CompilerParams(dimension_semantics=sem, vmem_limit_bytes=V7X_VMEM_LIMIT_BYTES)


def _rms(x, g):
    return x * lax.rsqrt(jnp.mean(x * x, axis=-1, keepdims=True) + NORM_EPS) * g


def _proj_kernel(*refs, prologue, glu, has_bias, headnorm, has_res, tn, s5_chunk):
    it = iter(refs)
    x_ref = next(it)
    yg_ref = next(it) if prologue == "s5_act" else None
    p1_ref = next(it) if prologue != "none" else None
    p2_ref = next(it) if prologue == "ln_silu" else None
    w_ref = next(it)
    b_ref = next(it) if has_bias else None
    hg_ref = next(it) if headnorm else None
    r_ref = next(it) if has_res else None
    o_ref = next(it)
    ug_ref = next(it) if (s5_chunk and prologue != "s5_act") else None
    xn_ref = next(it) if prologue != "none" else None

    if prologue == "s5_act":
        _s5_ungroup_tile(yg_ref, x_ref, p1_ref, xn_ref, s5_chunk)
        lhs_ref = xn_ref
    elif prologue != "none":
        x = x_ref[...].astype(F32)
        if prologue == "rms":
            y = _rms(x, p1_ref[...])
        else:
            mu = jnp.mean(x, axis=-1, keepdims=True)
            xc = x - mu
            y = xc * lax.rsqrt(jnp.mean(xc * xc, axis=-1, keepdims=True) + NORM_EPS)
            y = y * p1_ref[...] + p2_ref[...]
            y = y * jax.nn.sigmoid(y)
        xn_ref[...] = y.astype(BF16)
        lhs_ref = xn_ref
    else:
        lhs_ref = x_ref

    n_out = o_ref.shape[1]
    for j in range(n_out // tn):
        sl = slice(j * tn, (j + 1) * tn)
        xn = lhs_ref[...].astype(BF16)
        acc = jnp.dot(xn, w_ref[:, sl], preferred_element_type=F32)
        if has_bias:
            acc = acc + b_ref[:, sl]
        if glu:
            slg = slice(n_out + j * tn, n_out + (j + 1) * tn)
            gate = jnp.dot(xn, w_ref[:, slg], preferred_element_type=F32)
            if has_bias:
                gate = gate + b_ref[:, slg]
            acc = acc * jax.nn.sigmoid(gate)
        if has_res:
            acc = acc + r_ref[:, sl]
        if headnorm and j * tn < hg_ref.shape[1]:
            for h in range(tn // NA_HEAD_DIM):
                hs = slice(j * tn + h * NA_HEAD_DIM, j * tn + (h + 1) * NA_HEAD_DIM)
                hl = slice(h * NA_HEAD_DIM, (h + 1) * NA_HEAD_DIM)
                o_ref[:, hs] = _rms(acc[:, hl], hg_ref[:, hs]).astype(o_ref.dtype)
        else:
            o_ref[:, sl] = acc.astype(o_ref.dtype)
    if ug_ref is not None:
        _s5_group_tile(o_ref, ug_ref, s5_chunk)


def _proj(x, w, *, n_out, prologue="none", p1=None, p2=None, glu=False, bias=None,
          head_gain=None, res=None, out_dtype=F32, tm=512, tn=512, name="proj",
          s5_chunk=0, s5_grouped=None, s5_group_out=None):
    M, K = x.shape
    tm = min(tm, M)
    tn = min(tn, n_out)
    row = pl.BlockSpec((tm, K), lambda i: (i, 0))
    orow = pl.BlockSpec((tm, n_out), lambda i: (i, 0))
    const = lambda a: pl.BlockSpec(a.shape, lambda i: (0,) * a.ndim, pipeline_mode=pl.Buffered(1))
    args, specs = [x], [row]

    def add_const(a):
        args.append(a); specs.append(const(a))

    if prologue == "s5_act":
        args.append(s5_grouped[0]); specs.append(s5_grouped[1])
        add_const(p1.astype(F32))
    elif prologue != "none":
        add_const(p1.reshape(1, K).astype(F32))
    if prologue == "ln_silu":
        add_const(p2.reshape(1, K).astype(F32))
    add_const(w)
    if bias is not None:
        add_const(bias.reshape(1, -1).astype(F32))
    if head_gain is not None:
        add_const(head_gain.reshape(1, -1).astype(F32))
    if res is not None:
        args.append(res); specs.append(orow)
    scratch = [pltpu.VMEM((tm, K), BF16)] if prologue != "none" else []
    kern = functools.partial(_proj_kernel, prologue=prologue, glu=glu, has_bias=bias is not None,
                             headnorm=head_gain is not None, has_res=res is not None, tn=tn,
                             s5_chunk=s5_chunk)
    out_specs, out_shape = orow, jax.ShapeDtypeStruct((M, n_out), out_dtype)
    if s5_group_out is not None:
        out_specs = [orow, s5_group_out[1]]
        out_shape = [out_shape, jax.ShapeDtypeStruct(s5_group_out[0], F32)]
    return pl.pallas_call(
        kern,
        grid=(M // tm,),
        in_specs=specs,
        out_specs=out_specs,
        out_shape=out_shape,
        scratch_shapes=scratch,
        compiler_params=_params("parallel"),
        name=name,
    )(*args)


def _mlp_kernel(x_ref, g_ref, w1_ref, w2_ref, o_ref, xn_ref):
    @pl.when(pl.program_id(1) == 0)
    def _():
        x = x_ref[...]
        xn_ref[...] = _rms(x, g_ref[...]).astype(BF16)
        o_ref[...] = x

    h = jnp.dot(xn_ref[...], w1_ref[...], preferred_element_type=F32)
    h = jnp.square(jnp.maximum(h, 0.0)).astype(BF16)
    o_ref[...] += jnp.dot(h, w2_ref[...], preferred_element_type=F32)


def _mlp(x, g, w1, w2, layer, *, tm=1024, tf=512):
    M, D = x.shape
    F = w1.shape[2]
    tm = min(tm, M)
    tf = min(tf, F)
    return pl.pallas_call(
        _mlp_kernel,
        grid=(M // tm, F // tf),
        in_specs=[pl.BlockSpec((tm, D), lambda i, f: (i, 0)),
                  pl.BlockSpec((1, D), lambda i, f: (0, 0)),
                  pl.BlockSpec((None, D, tf), lambda i, f: (layer, 0, f)),
                  pl.BlockSpec((None, tf, D), lambda i, f: (layer, f, 0))],
        out_specs=pl.BlockSpec((tm, D), lambda i, f: (i, 0)),
        out_shape=jax.ShapeDtypeStruct((M, D), F32),
        scratch_shapes=[pltpu.VMEM((tm, D), BF16)],
        compiler_params=_params("parallel", "arbitrary"),
        name="mlp",
    )(x, g.reshape(1, D).astype(F32), w1, w2)


def _dft_tables(n, scale, rows=None):
    k = jnp.arange(n if rows is None else rows, dtype=jnp.int32)
    s = jnp.arange(n, dtype=jnp.int32)
    ang = ((k[:, None] * s[None, :]) % n).astype(F32) * (2.0 * math.pi / n)
    return jnp.cos(ang) * scale, jnp.sin(ang) * scale


def _fnet_chan_kernel(x_ref, g_ref, t_ref, a_ref, b_ref):
    gc = t_ref.shape[0]
    xn = _rms(x_ref[...], g_ref[...]).astype(BF16)
    for g in range(xn.shape[1] // gc):
        sl = slice(g * gc, (g + 1) * gc)
        r = jnp.dot(xn[:, sl], t_ref[...], preferred_element_type=F32)
        a_ref[:, sl] = r[:, :gc].astype(BF16)
        b_ref[:, sl] = r[:, gc:].astype(BF16)


def _fnet_chan(x, g, table, *, tm=512):
    M, D = x.shape
    gc = table.shape[0]
    tm = min(tm, M)
    row = pl.BlockSpec((tm, D), lambda i: (i, 0))
    return pl.pallas_call(
        _fnet_chan_kernel,
        grid=(M // tm,),
        in_specs=[row, pl.BlockSpec((1, D), lambda i: (0, 0)),
                  pl.BlockSpec((gc, 2 * gc), lambda i: (0, 0))],
        out_specs=[row, row],
        out_shape=[jax.ShapeDtypeStruct((M, D), BF16)] * 2,
        compiler_params=_params("parallel"),
        name="fnet_chan",
    )(x, g.reshape(1, D).astype(F32), table)


def _fnet_seq_kernel(cs_ref, ss_ref, cmid_ref, rev_ref, a_ref, b_ref, z_ref):
    half = cs_ref.shape[0]
    a = a_ref[...]
    pm = jnp.dot(cs_ref[...], a, preferred_element_type=F32)
    qm = jnp.dot(ss_ref[...], b_ref[...], preferred_element_type=F32)
    z_ref[0:half, :] = (pm - qm).astype(z_ref.dtype)
    upper = jnp.dot(rev_ref[...], (pm + qm).astype(BF16), preferred_element_type=F32)
    mid = jnp.dot(cmid_ref[...], a, preferred_element_type=F32)[0:1, :]
    row = lax.broadcasted_iota(jnp.int32, (half, 1), 0)
    z_ref[half:2 * half, :] = jnp.where(row == 0, mid, upper).astype(z_ref.dtype)


def _fnet_seq(cs, ss, a, b, *, tn=512):
    B, S, D = a.shape
    half = S // 2
    tn = min(tn, D)
    k = jnp.arange(half, dtype=jnp.int32)
    rev = (k[:, None] + k[None, :] == half).astype(BF16)
    const = lambda arr: pl.BlockSpec(arr.shape, lambda bi, j: (0, 0))
    colblk = pl.BlockSpec((None, S, tn), lambda bi, j: (bi, 0, j))
    tabs = [cs[:half], ss[:half], cs[half:half + SUBLANES], rev]
    return pl.pallas_call(
        _fnet_seq_kernel,
        grid=(B, D // tn),
        in_specs=[const(t) for t in tabs] + [colblk, colblk],
        out_specs=colblk,
        out_shape=jax.ShapeDtypeStruct((B, S, D), BF16),
        compiler_params=_params("parallel", "arbitrary"),
        name="fnet_seq",
    )(*tabs, a, b)


def _fourier_layer(x, norm_g, w, b, B, S):
    M, D = x.shape
    gc = D // FNET_GROUPS
    cc, sc = _dft_tables(gc, gc ** -0.5)
    cs, ss = _dft_tables(S, S ** -0.5, rows=S // 2 + SUBLANES)
    table = jnp.concatenate([cc, sc], axis=1).astype(BF16)
    a, bm = _fnet_chan(x, norm_g, table)
    z = _fnet_seq(cs.astype(BF16), ss.astype(BF16), a.reshape(B, S, D), bm.reshape(B, S, D))
    return _proj(z.reshape(M, D), w.astype(BF16), n_out=D, bias=b, res=x, name="fnet_out")


CONV_HALO = 16
CONV_ROWS = 128


def _dwconv_kernel(u_ref, w_ref, b_ref, o_ref, pad_ref):
    S, tc = u_ref.shape
    zeros = jnp.zeros((CONV_HALO, tc), F32)
    pad_ref[0:CONV_HALO, :] = zeros
    pad_ref[CONV_HALO + S:2 * CONV_HALO + S, :] = zeros
    pad_ref[CONV_HALO:CONV_HALO + S, :] = u_ref[...]
    first = CONV_HALO - CONV_WIDTH // 2

    def body(r, carry):
        base = pl.multiple_of(r * CONV_ROWS, CONV_ROWS)
        acc = jnp.broadcast_to(b_ref[...], (CONV_ROWS, tc))
        for res in range(SUBLANES):
            nrows = CONV_ROWS + (SUBLANES if res else 0)
            part = None
            for k in range(CONV_WIDTH):
                off = first + k
                if off % SUBLANES == res:
                    start = pl.multiple_of(base + (off - res), SUBLANES)
                    term = pad_ref[pl.ds(start, nrows), :] * w_ref[k:k + 1, :]
                    part = term if part is None else part + term
            if part is not None:
                acc = acc + part[res:res + CONV_ROWS, :]
        o_ref[pl.ds(base, CONV_ROWS), :] = acc
        return carry

    lax.fori_loop(0, S // CONV_ROWS, body, 0)


def _dwconv(u, w_dw, b_dw, *, tc=256):
    B, S, D = u.shape
    tc = min(tc, D)
    blk = pl.BlockSpec((None, S, tc), lambda bi, c: (bi, 0, c))
    return pl.pallas_call(
        _dwconv_kernel,
        grid=(B, D // tc),
        in_specs=[blk, pl.BlockSpec((CONV_WIDTH, tc), lambda bi, c: (0, c)),
                  pl.BlockSpec((1, tc), lambda bi, c: (0, c))],
        out_specs=blk,
        out_shape=jax.ShapeDtypeStruct((B, S, D), F32),
        scratch_shapes=[pltpu.VMEM((S + 2 * CONV_HALO, tc), F32)],
        compiler_params=_params("parallel", "parallel"),
        name="dwconv",
    )(u, w_dw.astype(F32), b_dw.reshape(1, D).astype(F32))


def _conv_layer(x, norm_g, w_in, b_in, w_dw, b_dw, ln_g, ln_b, w_out, b_out, B, S):
    M, D = x.shape
    u = _proj(x, w_in.astype(BF16), n_out=D, prologue="rms", p1=norm_g, glu=True, bias=b_in,
              name="conv_in")
    v = _dwconv(u.reshape(B, S, D), w_dw, b_dw).reshape(M, D)
    return _proj(v, w_out.astype(BF16), n_out=D, prologue="ln_silu", p1=ln_g, p2=ln_b,
                 bias=b_out, res=x, name="conv_out")


def _na_key_row_starts(rows):
    nblk = rows // NA_QROWS
    return [int(np.clip(i * NA_QROWS - NA_WIN_H // 2, 0, rows - NA_KROWS)) for i in range(nblk)]


NA_TAB = 18 * GRID_W


def _na_column_bias(rpb):
    H, ndr, _ = rpb.shape
    qc = np.arange(GRID_W)
    kc = np.arange(GRID_W)
    cstart = np.clip(qc - NA_WIN_W // 2, 0, GRID_W - NA_WIN_W)
    vcol = (kc[None, :] >= cstart[:, None]) & (kc[None, :] < cstart[:, None] + NA_WIN_W)
    dc = np.clip(kc[None, :] - qc[:, None] + NA_WIN_W - 1, 0, 2 * NA_WIN_W - 2)
    t = jnp.where(jnp.asarray(vcol)[None, None], rpb.astype(F32)[:, :, dc], MASK_VALUE)
    t = jnp.transpose(t, (0, 2, 1, 3)).reshape(H, GRID_W, ndr * GRID_W)
    pad = lambda n: jnp.full((H, GRID_W, n * GRID_W), MASK_VALUE, F32)
    nblk = NA_TAB // GRID_W
    tab_a = jnp.concatenate([t, pad(nblk - ndr)], axis=-1)
    tab_b = jnp.concatenate([pad(1), t, pad(nblk - ndr - 1)], axis=-1)
    return jnp.concatenate([tab_a, tab_b], axis=-1)


def _na_build_bias(tab_ref, bias_ref, key_starts, rows):
    kh = min(NA_WIN_H, rows)
    pair = 2 * GRID_W
    lane = lax.broadcasted_iota(jnp.int32, (GRID_W, pair), 1)
    masked = jnp.full((GRID_W, pair), MASK_VALUE, F32)
    for i, ks in enumerate(key_starts):
        for qr in range(NA_QROWS):
            r = i * NA_QROWS + qr
            r0 = int(np.clip(r - kh // 2, 0, rows - kh))
            for kp in range(NA_KROWS // 2):
                kabs = ks + 2 * kp
                v0 = r0 <= kabs < r0 + kh
                v1 = r0 <= kabs + 1 < r0 + kh
                dr = kabs - r + NA_WIN_H - 1
                if not (v0 or v1):
                    tile = masked
                else:
                    start = dr * GRID_W if dr % 2 == 0 else NA_TAB + (dr + 1) * GRID_W
                    tile = tab_ref[:, start:start + pair]
                    if not v1:
                        tile = jnp.where(lane < GRID_W, tile, MASK_VALUE)
                    elif not v0:
                        tile = jnp.where(lane >= GRID_W, tile, MASK_VALUE)
                bias_ref[i, qr * GRID_W:(qr + 1) * GRID_W, kp * pair:(kp + 1) * pair] = tile


def _na_kernel(q_ref, k_ref, v_ref, tab_ref, o_ref, bias_ref, *, key_starts, rows):
    @pl.when(pl.program_id(1) == 0)
    def _():
        _na_build_bias(tab_ref, bias_ref, key_starts, rows)

    nq = NA_QROWS * GRID_W
    nk = NA_KROWS * GRID_W
    for i, ks in enumerate(key_starts):
        q = q_ref[i * nq:(i + 1) * nq, :]
        k = k_ref[ks * GRID_W:ks * GRID_W + nk, :]
        v = v_ref[ks * GRID_W:ks * GRID_W + nk, :]
        s = lax.dot_general(q, k, (((1,), (1,)), ((), ())), preferred_element_type=F32)
        s = s + bias_ref[i]
        p = jnp.exp(s - jnp.max(s, axis=-1, keepdims=True))
        l = jnp.sum(p, axis=-1, keepdims=True)
        o = jnp.dot(p.astype(BF16), v, preferred_element_type=F32)
        o_ref[i * nq:(i + 1) * nq, :] = (o / l).astype(o_ref.dtype)


def _na_attention(qkv, tab, B, S, H):
    M = qkv.shape[0]
    rows = S // GRID_W
    dh = NA_HEAD_DIM
    starts = tuple(_na_key_row_starts(rows))
    kern = functools.partial(_na_kernel, key_starts=starts, rows=rows)
    return pl.pallas_call(
        kern,
        grid=(H, B),
        in_specs=[pl.BlockSpec((S, dh), lambda h, b: (b, h)),
                  pl.BlockSpec((S, dh), lambda h, b: (b, H + h)),
                  pl.BlockSpec((S, dh), lambda h, b: (b, 2 * H + h)),
                  pl.BlockSpec((None,) + tab.shape[1:], lambda h, b: (h, 0, 0))],
        out_specs=pl.BlockSpec((S, dh), lambda h, b: (b, h)),
        out_shape=jax.ShapeDtypeStruct((M, H * dh), BF16),
        scratch_shapes=[pltpu.VMEM((len(starts), NA_QROWS * GRID_W, NA_KROWS * GRID_W), F32)],
        compiler_params=_params("arbitrary", "arbitrary"),
        name="na_attn",
    )(qkv, qkv, qkv, tab)


def _na_layer(x, norm_g, w_qkv, q_gain, k_gain, rpb, w_o, B, S):
    M, D = x.shape
    H = D // NA_HEAD_DIM
    rows = S // GRID_W
    assert rows % NA_QROWS == 0 and rows >= NA_KROWS
    wb = w_qkv.astype(BF16)
    head_gain = jnp.concatenate([jnp.tile(q_gain.astype(F32) * NA_HEAD_DIM ** -0.5, H),
                                 jnp.tile(k_gain.astype(F32), H)])
    qkv = _proj(x, wb, n_out=3 * D, prologue="rms", p1=norm_g, head_gain=head_gain,
                out_dtype=BF16, name="na_qkv")
    o = _na_attention(qkv, _na_column_bias(rpb), B, S, H)
    return _proj(o, w_o.astype(BF16), n_out=D, res=x, name="na_out")


def _s5_operators(a_re, a_im, log_dt, b_re, b_im, c_re, c_im):
    T = S5_CHUNK
    G, P = a_re.shape[1:]
    Cg = b_re.shape[-1]
    lam_re = jnp.minimum(a_re.astype(F32), -1e-4)
    lam_im = a_im.astype(F32)
    dt = jnp.exp(log_dt.astype(F32))[..., None]
    k = jnp.arange(T + 1, dtype=F32)[:, None, None, None]
    mag = jnp.exp(k * (lam_re * dt)[None])
    pw_r = mag * jnp.cos(k * (lam_im * dt)[None])
    pw_i = mag * jnp.sin(k * (lam_im * dt)[None])
    ab_r, ab_i = pw_r[1], pw_i[1]
    den = jnp.square(lam_re) + jnp.square(lam_im)
    w_r, w_i = ab_r - 1.0, ab_i
    f_r = (w_r * lam_re + w_i * lam_im) / den
    f_i = (w_i * lam_re - w_r * lam_im) / den
    br, bi = b_re.astype(F32), b_im.astype(F32)
    bb_r = f_r[..., None] * br - f_i[..., None] * bi
    bb_i = f_r[..., None] * bi + f_i[..., None] * br
    cr, ci = c_re.astype(F32), c_im.astype(F32)
    ca_r = cr[None] * pw_r[:, :, :, None, :] - ci[None] * pw_i[:, :, :, None, :]
    ca_i = cr[None] * pw_i[:, :, :, None, :] + ci[None] * pw_r[:, :, :, None, :]
    kern = (jnp.einsum("kdgcp,dgpe->dgeck", ca_r[:T], bb_r)
            - jnp.einsum("kdgcp,dgpe->dgeck", ca_i[:T], bb_i))
    kf, kb = kern[0], kern[1]
    lags = jnp.concatenate([jnp.flip(kb[..., 1:], axis=-1), kf[..., :1] + kb[..., :1], kf[..., 1:]], axis=-1)
    m = jnp.stack([lags[..., T - 1 - s:2 * T - 1 - s] for s in range(T)], axis=2)
    m = m.reshape(G, Cg * T, Cg * T)
    def state_in(d, reverse):
        pr, pi = pw_r[:T, d], pw_i[:T, d]
        if reverse:
            pr, pi = jnp.flip(pr, axis=0), jnp.flip(pi, axis=0)
        e_r = pr[..., None] * bb_r[d][None] - pi[..., None] * bb_i[d][None]
        e_i = pr[..., None] * bb_i[d][None] + pi[..., None] * bb_r[d][None]
        tr = lambda e: jnp.transpose(e, (1, 3, 0, 2)).reshape(G, Cg * T, P)
        return tr(e_r), tr(e_i)

    ef_r, ef_i = state_in(0, True)
    eb_r, eb_i = state_in(1, False)
    e = jnp.concatenate([ef_r, eb_r, ef_i, eb_i], axis=-1)
    tr = lambda e: jnp.transpose(e, (1, 3, 2, 0)).reshape(G, P, Cg * T)
    fwd = lambda ca: tr(ca[1:T + 1, 0])
    bwd = lambda ca: tr(jnp.flip(ca[1:T + 1, 1], axis=0))
    w2r = jnp.concatenate([fwd(ca_r), bwd(ca_r)], axis=1)
    w2i = jnp.concatenate([-fwd(ca_i), -bwd(ca_i)], axis=1)
    at_r = jnp.concatenate([pw_r[T, 0], pw_r[T, 1]], axis=-1)[:, None, :]
    at_i = jnp.concatenate([pw_i[T, 0], pw_i[T, 1]], axis=-1)[:, None, :]
    return m.astype(BF16), e.astype(BF16), w2r.astype(BF16), w2i.astype(BF16), at_r, at_i


S5_GROUP_BLOCK = 8
S5_BATCH_BLOCK = 8


def _s5_kernel(u_ref, m_ref, e_ref, w2r_ref, w2i_ref, ar_ref, ai_ref, y_ref, p_ref, xr_ref, xi_ref):
    nchunk, nl, nbg, _ = u_ref.shape
    TC = nl * LANES
    ng = m_ref.shape[0]
    nb = nbg // ng
    R = nchunk * nb
    P2 = ar_ref.shape[-1]
    P = P2 // 2
    zeros = jnp.zeros((nb, P), F32)
    last = (nchunk - 1) * nb
    is_fwd = lax.broadcasted_iota(jnp.int32, (nb, P2), 1) < P
    for gi in range(ng):
        rows_g = pl.ds(gi, nb, stride=ng)
        lhs = jnp.concatenate([u_ref[:, pl.ds(l, 1), rows_g, :].reshape(R, LANES) for l in range(nl)], axis=-1)
        lhs = lhs.astype(BF16)
        p_ref[:, 0:TC] = jnp.dot(lhs, m_ref[gi], preferred_element_type=F32)
        p_ref[:, TC:] = jnp.dot(lhs, e_ref[gi], preferred_element_type=F32)
        xr_ref[0:nb, 0:P] = zeros
        xi_ref[0:nb, 0:P] = zeros
        xr_ref[last:last + nb, P:P2] = zeros
        xi_ref[last:last + nb, P:P2] = zeros
        ar = jnp.broadcast_to(ar_ref[gi], (nb, P2))
        ai = jnp.broadcast_to(ai_ref[gi], (nb, P2))

        def body(j, carry):
            xr, xi = carry
            rf = pl.multiple_of(j * nb, nb)
            rb = pl.multiple_of((nchunk - 1 - j) * nb, nb)
            in_r = jnp.where(is_fwd, p_ref[pl.ds(rf, nb), TC:TC + P2], p_ref[pl.ds(rb, nb), TC:TC + P2])
            in_i = jnp.where(is_fwd, p_ref[pl.ds(rf, nb), TC + P2:TC + 2 * P2],
                             p_ref[pl.ds(rb, nb), TC + P2:TC + 2 * P2])
            nxr = ar * xr - ai * xi + in_r
            nxi = ar * xi + ai * xr + in_i
            wf = pl.multiple_of((j + 1) * nb, nb)
            wb = pl.multiple_of((nchunk - 2 - j) * nb, nb)
            xr_ref[pl.ds(wf, nb), 0:P] = nxr[:, 0:P]
            xi_ref[pl.ds(wf, nb), 0:P] = nxi[:, 0:P]
            xr_ref[pl.ds(wb, nb), P:P2] = nxr[:, P:P2]
            xi_ref[pl.ds(wb, nb), P:P2] = nxi[:, P:P2]
            return nxr, nxi

        z = jnp.zeros((nb, P2), F32)
        lax.fori_loop(0, nchunk - 1, body, (z, z))
        y = p_ref[:, 0:TC]
        y = y + jnp.dot(xr_ref[...].astype(BF16), w2r_ref[gi], preferred_element_type=F32)
        y = y + jnp.dot(xi_ref[...].astype(BF16), w2i_ref[gi], preferred_element_type=F32)
        for l in range(nl):
            y_ref[:, pl.ds(l, 1), rows_g, :] = y[:, l * LANES:(l + 1) * LANES].reshape(nchunk, 1, nb, LANES)


def _s5_scan(ug, m, e, w2r, w2i, at_r, at_i):
    nchunk, nbb, ngb, nl, nbg, _ = ug.shape
    P2 = at_r.shape[-1]
    ng = m.shape[0] // ngb
    R = nchunk * (nbg // ng)
    blk = lambda a: pl.BlockSpec((ng,) + a.shape[1:], lambda g, b: (g, 0, 0))
    grp = pl.BlockSpec((nchunk, None, None, nl, nbg, LANES), lambda g, b: (0, b, g, 0, 0, 0))
    return pl.pallas_call(
        _s5_kernel,
        grid=(ngb, nbb),
        in_specs=[grp, blk(m), blk(e), blk(w2r), blk(w2i), blk(at_r), blk(at_i)],
        out_specs=grp,
        out_shape=jax.ShapeDtypeStruct(ug.shape, F32),
        scratch_shapes=[pltpu.VMEM((R, m.shape[-1] + e.shape[-1]), F32), pltpu.VMEM((R, P2), F32),
                        pltpu.VMEM((R, P2), F32)],
        compiler_params=_params("parallel", "arbitrary"),
        name="s5_scan",
    )(ug, m, e, w2r, w2i, at_r, at_i)


def _s5_group_tile(u_ref, o_ref, T):
    nc, ngb, nl, ng, _ = o_ref.shape
    TC = nl * LANES
    G = ngb * ng
    for ci in range(nc):
        rows = slice(ci * T, (ci + 1) * T)
        tile = jnp.concatenate([u_ref[rows, c * G:(c + 1) * G] for c in range(TC // T)], axis=0)
        t = tile.T.reshape(ngb, ng, TC)
        for l in range(nl):
            o_ref[ci, :, l] = t[:, :, l * LANES:(l + 1) * LANES]


def _s5_ungroup_tile(y_ref, u_ref, d_ref, o_ref, T):
    nc, ngb, nl, ng, _ = y_ref.shape
    TC = nl * LANES
    G = ngb * ng
    for ci in range(nc):
        rows = slice(ci * T, (ci + 1) * T)
        t = jnp.concatenate([y_ref[ci, :, l] for l in range(nl)], axis=-1)
        yt = t.reshape(G, TC).T
        for c in range(TC // T):
            cols = slice(c * G, (c + 1) * G)
            y = yt[c * T:(c + 1) * T, :] + d_ref[c:c + 1, :] * u_ref[rows, cols]
            o_ref[rows, cols] = jax.nn.gelu(y).astype(o_ref.dtype)


def _s5_grouped(B, nchunk, T, G, Cg, tm):
    TC = Cg * T
    nb = min(S5_BATCH_BLOCK, B)
    ng = min(S5_GROUP_BLOCK, G)
    nc = tm // T
    steps = nchunk // nc

    def index(i):
        b, c = i // steps, i % steps
        return (c, b // nb, 0, 0, b % nb, 0)

    shape = (nchunk, B // nb, G // ng, TC // LANES, nb * ng, LANES)
    return shape, pl.BlockSpec((nc, None, G // ng, TC // LANES, ng, LANES), index)


def _s5_layer(x, norm_g, w_in, a_re, a_im, log_dt, b_re, b_im, c_re, c_im, d_skip, w_glu, B, S, *, tm=512):
    M, D = x.shape
    T = S5_CHUNK
    G, Cg = d_skip.shape
    tm = min(tm, S)
    w_in_p = w_in.reshape(D, G, Cg).transpose(0, 2, 1).reshape(D, D).astype(BF16)
    w_glu_p = w_glu.reshape(G, Cg, -1).transpose(1, 0, 2).reshape(D, -1).astype(BF16)
    grouped = _s5_grouped(B, S // T, T, G, Cg, tm)
    u, ug = _proj(x, w_in_p, n_out=D, prologue="rms", p1=norm_g, tm=tm, name="s5_in",
                  s5_chunk=T, s5_group_out=grouped)
    ops = _s5_operators(a_re, a_im, log_dt, b_re, b_im, c_re, c_im)
    yg = _s5_scan(ug, *ops)
    return _proj(u, w_glu_p, n_out=D, prologue="s5_act", p1=d_skip.T, glu=True, res=x, tm=tm,
                 name="s5_out", s5_chunk=T, s5_grouped=(yg, grouped[1]))


def kernel(x, mix_norm, fnet_w, fnet_b, conv_w_in, conv_b_in, conv_w_dw, conv_b_dw, conv_ln_g,
           conv_ln_b, conv_w_out, conv_b_out, na_w_qkv, na_q_gain, na_k_gain, na_rpb, na_w_o,
           s5_w_in, s5_a_re, s5_a_im, s5_log_dt, s5_b_re, s5_b_im, s5_c_re, s5_c_im, s5_d,
           s5_w_glu, mlp_norm, mlp_w1, mlp_w2):
    B, S, D = x.shape
    depth = mix_norm.shape[0]
    h = x.reshape(B * S, D)
    w1_bf, w2_bf = mlp_w1.astype(BF16), mlp_w2.astype(BF16)
    for i in range(depth):
        kind, j = i % 4, i // 4
        if kind == 0:
            h = _fourier_layer(h, mix_norm[i], fnet_w[j], fnet_b[j], B, S)
        elif kind == 1:
            h = _conv_layer(h, mix_norm[i], conv_w_in[j], conv_b_in[j], conv_w_dw[j], conv_b_dw[j],
                            conv_ln_g[j], conv_ln_b[j], conv_w_out[j], conv_b_out[j], B, S)
        elif kind == 2:
            h = _na_layer(h, mix_norm[i], na_w_qkv[j], na_q_gain[j], na_k_gain[j], na_rpb[j],
                          na_w_o[j], B, S)
        else:
            h = _s5_layer(h, mix_norm[i], s5_w_in[j], s5_a_re[j], s5_a_im[j], s5_log_dt[j],
                          s5_b_re[j], s5_b_im[j], s5_c_re[j], s5_c_im[j], s5_d[j], s5_w_glu[j], B, S)
        h = _mlp(h, mlp_norm[i], w1_bf, w2_bf, i)
    return h.reshape(B, S, D)
```
